```python
import jax, jax.numpy as jnp
from jax import lax
import numpy as np

D_MODEL = 2048
BATCH = 4
SEQ = 2048
DEPTH = 4
DEC_BATCH = 8
DEC_SEQ = 8
PAST_LEN = 16384
PAGE_SIZE = 128

N_MIXERS = 2
N_SSD_LAYERS = (DEPTH + 1) // 2
N_ATT_LAYERS = DEPTH // 2
SSD_EXPAND = 2
D_INNER = SSD_EXPAND * D_MODEL
SSD_HEAD_DIM = 64
SSD_HEADS = D_INNER // SSD_HEAD_DIM
SSD_GROUPS = 8
D_STATE = 128
CONV_W = 4
CONV_DIM = D_INNER + 2 * SSD_GROUPS * D_STATE
SSD_IN_DIM = D_INNER + CONV_DIM + SSD_HEADS
SSD_CHUNK = 128
FOX_HEADS = 16
FOX_HEAD_DIM = D_MODEL // FOX_HEADS
FOX_IN_DIM = 3 * D_MODEL + FOX_HEADS
Q_BLOCK = 128
FORGET_BIAS = 8.0
D_FF = ((8 * D_MODEL + 3 * 256 - 1) // (3 * 256)) * 256
RMS_EPS = 1e-6

kernel_name = 'hybrid_ssd_fox_decoder_step'


def rmsnorm(x, g):
    xf = x.astype(jnp.float32)
    y = xf * lax.rsqrt(jnp.mean(xf * xf, axis=-1, keepdims=True) + RMS_EPS)
    return (y * g.astype(jnp.float32)).astype(x.dtype)


def segsum(a):
    t = a.shape[-1]
    xx = jnp.broadcast_to(a[..., :, None], a.shape + (t,))
    xx = jnp.where(jnp.tril(jnp.ones((t, t), bool), -1), xx, 0.0)
    s = jnp.cumsum(xx, axis=-2)
    return jnp.where(jnp.tril(jnp.ones((t, t), bool), 0), s, -jnp.inf)


def ssd_scan(X, A, B, C, h0, chunk):
    b, l, h, p = X.shape
    g, n = B.shape[2], B.shape[3]
    r = h // g
    c = l // chunk
    Xc = X.reshape(b, c, chunk, g, r, p)
    Ac = A.reshape(b, c, chunk, g, r).transpose(0, 3, 4, 1, 2)
    Bc = B.reshape(b, c, chunk, g, n)
    Cc = C.reshape(b, c, chunk, g, n)
    A_cum = jnp.cumsum(Ac, axis=-1)
    Lmat = jnp.exp(segsum(Ac))
    CB = jnp.einsum('bclgn,bcsgn->bcgls', Cc, Bc)
    Y_diag = jnp.einsum('bcgls,bgrcls,bcsgrp->bclgrp', CB, Lmat, Xc)
    decay_states = jnp.exp(A_cum[..., -1:] - A_cum)
    states = jnp.einsum('bclgn,bgrcl,bclgrp->bcgrpn', Bc, decay_states, Xc)
    states = jnp.concatenate([h0.reshape(b, 1, g, r, p, n), states], axis=1)
    chunk_tot = jnp.pad(A_cum[..., -1], ((0, 0), (0, 0), (0, 0), (1, 0)))
    decay_chunk = jnp.exp(segsum(chunk_tot))
    new_states = jnp.einsum('bgrzc,bcgrpn->bzgrpn', decay_chunk, states)
    start_states, final = new_states[:, :-1], new_states[:, -1]
    Y_off = jnp.einsum('bclgn,bcgrpn,bgrcl->bclgrp', Cc, start_states, jnp.exp(A_cum))
    Y = (Y_diag + Y_off).reshape(b, l, h, p)
    return Y, final.reshape(b, h, p, n)


def ssd_mixer(xn, conv_state, ssm_state, w_in, conv_w, conv_b, dt_bias, a_log, d_skip, norm_g, w_out):
    b, l, _ = xn.shape
    f32 = jnp.float32
    zxbcdt = xn @ w_in
    z = zxbcdt[..., :D_INNER]
    xbc = zxbcdt[..., D_INNER:D_INNER + CONV_DIM]
    dt = zxbcdt[..., D_INNER + CONV_DIM:]
    xpad = jnp.concatenate([conv_state.astype(xbc.dtype), xbc], axis=1)
    conv = conv_b + sum(xpad[:, k:k + l] * conv_w[k] for k in range(CONV_W))
    new_conv = xpad[:, -(CONV_W - 1):]
    xbc = jax.nn.silu(conv)
    gn = SSD_GROUPS * D_STATE
    xs = xbc[..., :D_INNER].reshape(b, l, SSD_HEADS, SSD_HEAD_DIM).astype(f32)
    Bm = xbc[..., D_INNER:D_INNER + gn].reshape(b, l, SSD_GROUPS, D_STATE).astype(f32)
    Cm = xbc[..., D_INNER + gn:].reshape(b, l, SSD_GROUPS, D_STATE).astype(f32)
    dt = jax.nn.softplus(dt.astype(f32) + dt_bias.astype(f32))
    A = -jnp.exp(a_log.astype(f32))
    chunk = SSD_CHUNK if l % SSD_CHUNK == 0 else l
    y, new_ssm = ssd_scan(xs * dt[..., None], dt * A, Bm, Cm, ssm_state.astype(f32), chunk)
    y = y + xs * d_skip.astype(f32)[:, None]
    y = y.reshape(b, l, D_INNER) * jax.nn.silu(z.astype(f32))
    yg = y.reshape(b, l, SSD_GROUPS, D_INNER // SSD_GROUPS)
    yg = yg * lax.rsqrt(jnp.mean(yg * yg, axis=-1, keepdims=True) + RMS_EPS)
    y = yg.reshape(b, l, D_INNER) * norm_g.astype(f32)
    out = y.astype(xn.dtype) @ w_out
    return out, new_conv, new_ssm


def fox_project(xn, w_in, b_f):
    b, l, _ = xn.shape
    proj = xn @ w_in
    q = proj[..., :D_MODEL].reshape(b, l, FOX_HEADS, FOX_HEAD_DIM)
    k = proj[..., D_MODEL:2 * D_MODEL].reshape(b, l, FOX_HEADS, FOX_HEAD_DIM)
    v = proj[..., 2 * D_MODEL:3 * D_MODEL].reshape(b, l, FOX_HEADS, FOX_HEAD_DIM)
    logf = jax.nn.log_sigmoid(proj[..., 3 * D_MODEL:].astype(jnp.float32) + b_f.astype(jnp.float32))
    return q, k, v, logf


def fox_attend(q, fq, qpos, k, v, fk, kpos):
    s = jnp.einsum('bqhd,bkhd->bhqk', q, k, preferred_element_type=jnp.float32) * (FOX_HEAD_DIM ** -0.5)
    s = s + jnp.swapaxes(fq, 1, 2)[..., :, None] - jnp.swapaxes(fk, 1, 2)[..., None, :]
    s = jnp.where(kpos[None, :] <= qpos[:, None], s, -jnp.inf)
    p = jax.nn.softmax(s, axis=-1)
    return jnp.einsum('bhqk,bkhd->bqhd', p.astype(v.dtype), v)


def fox_prompt(xn, w_in, b_f, w_out):
    b, l, _ = xn.shape
    q, k, v, logf = fox_project(xn, w_in, b_f)
    F = jnp.cumsum(logf, axis=1)
    pos = jnp.arange(l)
    nb = l // Q_BLOCK
    qb = jnp.swapaxes(q.reshape(b, nb, Q_BLOCK, FOX_HEADS, FOX_HEAD_DIM), 0, 1)
    Fb = jnp.swapaxes(F.reshape(b, nb, Q_BLOCK, FOX_HEADS), 0, 1)
    pb = pos.reshape(nb, Q_BLOCK)
    out = lax.map(lambda a: fox_attend(a[0], a[1], a[2], k, v, F, pos), (qb, Fb, pb))
    out = jnp.swapaxes(out, 0, 1).reshape(b, l, D_MODEL)
    return out @ w_out, k, v, logf


def fox_sample(xn, pool_k, pool_v, pool_logf, page_table, w_in, b_f, w_out):
    b, l, _ = xn.shape
    q, k, v, logf = fox_project(xn, w_in, b_f)
    past = page_table.shape[1] * PAGE_SIZE
    k_past = pool_k[page_table].reshape(b, past, FOX_HEADS, FOX_HEAD_DIM).astype(k.dtype)
    v_past = pool_v[page_table].reshape(b, past, FOX_HEADS, FOX_HEAD_DIM).astype(v.dtype)
    lf_past = pool_logf[page_table].reshape(b, past, FOX_HEADS).astype(jnp.float32)
    F_past = -(lax.cumsum(lf_past, axis=1, reverse=True) - lf_past)
    F_new = jnp.cumsum(logf, axis=1)
    kk = jnp.concatenate([k_past, k], axis=1)
    vv = jnp.concatenate([v_past, v], axis=1)
    Fk = jnp.concatenate([F_past, F_new], axis=1)
    kpos = jnp.arange(past + l)
    qpos = past + jnp.arange(l)
    out = fox_attend(q, F_new, qpos, kk, vv, Fk, kpos).reshape(b, l, D_MODEL)
    return out @ w_out, k, v, logf


def swiglu(xn, w_gate, w_up, w_down):
    return (jax.nn.silu(xn @ w_gate) * (xn @ w_up)) @ w_down


def setup_inputs(seed: int = 0) -> dict:
    key = jax.random.key(seed)
    ks = jax.random.split(key, 32)
    f32 = jnp.float32
    n_pages = PAST_LEN // PAGE_SIZE
    n_used = DEC_BATCH * n_pages
    n_pool = n_used + max(1, n_used // 4)
    nrm = lambda k, shape, s: jax.random.normal(k, shape, f32) * s
    x_prompt = nrm(ks[0], (BATCH, SEQ, D_MODEL), 1.0)
    x_sample = nrm(ks[1], (DEC_BATCH, DEC_SEQ, D_MODEL), 1.0)
    cache_k = nrm(ks[2], (N_ATT_LAYERS, n_pool, PAGE_SIZE, FOX_HEADS, FOX_HEAD_DIM), 1.0)
    cache_v = nrm(ks[3], (N_ATT_LAYERS, n_pool, PAGE_SIZE, FOX_HEADS, FOX_HEAD_DIM), 1.0)
    cache_logf = jax.nn.log_sigmoid(FORGET_BIAS + nrm(ks[4], (N_ATT_LAYERS, n_pool, PAGE_SIZE, FOX_HEADS), 1.0))
    state_conv = nrm(ks[5], (N_SSD_LAYERS, DEC_BATCH, CONV_W - 1, CONV_DIM), 1.0)
    state_ssm = nrm(ks[6], (N_SSD_LAYERS, DEC_BATCH, SSD_HEADS, SSD_HEAD_DIM, D_STATE), 0.1)
    page_table = jax.random.permutation(ks[7], n_pool)[:n_used].reshape(DEC_BATCH, n_pages).astype(jnp.int32)
    gain = lambda k, shape: 1.0 + nrm(k, shape, 0.02)
    ln_mix_pre = gain(ks[8], (DEPTH, D_MODEL))
    ln_mix_post = gain(ks[9], (DEPTH, D_MODEL))
    ln_ffn_pre = gain(ks[10], (DEPTH, D_MODEL))
    ln_ffn_post = gain(ks[11], (DEPTH, D_MODEL))
    ssd_w_in = nrm(ks[12], (N_SSD_LAYERS, D_MODEL, SSD_IN_DIM), D_MODEL ** -0.5)
    ssd_conv_w = nrm(ks[13], (N_SSD_LAYERS, CONV_W, CONV_DIM), CONV_W ** -0.5)
    ssd_conv_b = nrm(ks[14], (N_SSD_LAYERS, CONV_DIM), 0.01)
    dt0 = jnp.exp(jax.random.uniform(ks[15], (N_SSD_LAYERS, SSD_HEADS), f32, np.log(1e-3), np.log(1e-1)))
    ssd_dt_bias = dt0 + jnp.log(-jnp.expm1(-dt0))
    ssd_a_log = jnp.log(jax.random.uniform(ks[16], (N_SSD_LAYERS, SSD_HEADS), f32, 1.0, 16.0))
    ssd_d = 1.0 + nrm(ks[17], (N_SSD_LAYERS, SSD_HEADS), 0.1)
    ssd_norm = gain(ks[18], (N_SSD_LAYERS, D_INNER))
    ssd_w_out = nrm(ks[19], (N_SSD_LAYERS, D_INNER, D_MODEL), D_INNER ** -0.5)
    fox_w_in = nrm(ks[20], (N_ATT_LAYERS, D_MODEL, FOX_IN_DIM), D_MODEL ** -0.5)
    fox_b_f = FORGET_BIAS + nrm(ks[21], (N_ATT_LAYERS, FOX_HEADS), 0.5)
    fox_w_out = nrm(ks[22], (N_ATT_LAYERS, D_MODEL, D_MODEL), D_MODEL ** -0.5)
    ffn_w_gate = nrm(ks[23], (DEPTH, D_MODEL, D_FF), D_MODEL ** -0.5)
    ffn_w_up = nrm(ks[24], (DEPTH, D_MODEL, D_FF), D_MODEL ** -0.5)
    ffn_w_down = nrm(ks[25], (DEPTH, D_FF, D_MODEL), D_FF ** -0.5)
    return {'x_prompt': x_prompt, 'x_sample': x_sample, 'cache_k': cache_k, 'cache_v': cache_v,
            'cache_logf': cache_logf, 'state_conv': state_conv, 'state_ssm': state_ssm,
            'page_table': page_table, 'ln_mix_pre': ln_mix_pre, 'ln_mix_post': ln_mix_post,
            'ln_ffn_pre': ln_ffn_pre, 'ln_ffn_post': ln_ffn_post, 'ssd_w_in': ssd_w_in,
            'ssd_conv_w': ssd_conv_w, 'ssd_conv_b': ssd_conv_b, 'ssd_dt_bias': ssd_dt_bias,
            'ssd_a_log': ssd_a_log, 'ssd_d': ssd_d, 'ssd_norm': ssd_norm, 'ssd_w_out': ssd_w_out,
            'fox_w_in': fox_w_in, 'fox_b_f': fox_b_f, 'fox_w_out': fox_w_out,
            'ffn_w_gate': ffn_w_gate, 'ffn_w_up': ffn_w_up, 'ffn_w_down': ffn_w_down}


def reference(x_prompt, x_sample, cache_k, cache_v, cache_logf, state_conv, state_ssm, page_table,
              ln_mix_pre, ln_mix_post, ln_ffn_pre, ln_ffn_post, ssd_w_in, ssd_conv_w, ssd_conv_b,
              ssd_dt_bias, ssd_a_log, ssd_d, ssd_norm, ssd_w_out, fox_w_in, fox_b_f, fox_w_out,
              ffn_w_gate, ffn_w_up, ffn_w_down):
    hp, hs = x_prompt, x_sample
    b_p = hp.shape[0]
    zero_conv = jnp.zeros((b_p, CONV_W - 1, CONV_DIM), hp.dtype)
    zero_ssm = jnp.zeros((b_p, SSD_HEADS, SSD_HEAD_DIM, D_STATE), jnp.float32)
    kp_l, vp_l, fp_l, cp_l, sp_l = [], [], [], [], []
    ks_l, vs_l, fs_l, cs_l, ss_l = [], [], [], [], []
    for i in range(DEPTH):
        j = i // N_MIXERS
        xp_n = rmsnorm(hp, ln_mix_pre[i])
        xs_n = rmsnorm(hs, ln_mix_pre[i])
        if i % N_MIXERS == 0:
            w = (ssd_w_in[j], ssd_conv_w[j], ssd_conv_b[j], ssd_dt_bias[j], ssd_a_log[j],
                 ssd_d[j], ssd_norm[j], ssd_w_out[j])
            mp, c_p, s_p = ssd_mixer(xp_n, zero_conv, zero_ssm, *w)
            ms, c_s, s_s = ssd_mixer(xs_n, state_conv[j], state_ssm[j], *w)
            cp_l.append(c_p); sp_l.append(s_p); cs_l.append(c_s); ss_l.append(s_s)
        else:
            mp, k_p, v_p, f_p = fox_prompt(xp_n, fox_w_in[j], fox_b_f[j], fox_w_out[j])
            ms, k_s, v_s, f_s = fox_sample(xs_n, cache_k[j], cache_v[j], cache_logf[j], page_table,
                                           fox_w_in[j], fox_b_f[j], fox_w_out[j])
            kp_l.append(k_p); vp_l.append(v_p); fp_l.append(f_p)
            ks_l.append(k_s); vs_l.append(v_s); fs_l.append(f_s)
        hp = hp + rmsnorm(mp, ln_mix_post[i])
        hs = hs + rmsnorm(ms, ln_mix_post[i])
        hp = hp + rmsnorm(swiglu(rmsnorm(hp, ln_ffn_pre[i]), ffn_w_gate[i], ffn_w_up[i], ffn_w_down[i]), ln_ffn_post[i])
        hs = hs + rmsnorm(swiglu(rmsnorm(hs, ln_ffn_pre[i]), ffn_w_gate[i], ffn_w_up[i], ffn_w_down[i]), ln_ffn_post[i])
    return (hp, hs,
            jnp.stack(kp_l), jnp.stack(vp_l), jnp.stack(fp_l), jnp.stack(cp_l), jnp.stack(sp_l),
            jnp.stack(ks_l), jnp.stack(vs_l), jnp.stack(fs_l), jnp.stack(cs_l), jnp.stack(ss_l))
```

```python
import functools

import jax
import jax.numpy as jnp
from jax import lax
from jax.experimental import pallas as pl
from jax.experimental.pallas import tpu as pltpu

F32 = jnp.float32
BF16 = jnp.bfloat16

RMS_EPS = 1e-6
LANES = 128
SUBLANES = 8
SSD_HEAD_DIM = 64
SSD_GROUPS = 8
D_STATE = 128
CONV_W = 4
SSD_CHUNK = 128
FOX_HEAD_DIM = 128
PAGE_SIZE = 128
VMEM_LIMIT = 56 * 1024 * 1024


def _cparams(sem):
    return pltpu.CompilerParams(dimension_semantics=sem, vmem_limit_bytes=VMEM_LIMIT)


def _sigmoid(x):
    return 1.0 / (1.0 + jnp.exp(-x))


def _softplus(x):
    return jnp.maximum(x, 0.0) + jnp.log1p(jnp.exp(-jnp.abs(x)))


def _rms(x, g):
    ms = jnp.mean(x * x, axis=-1, keepdims=True)
    return x * lax.rsqrt(ms + RMS_EPS) * g


def _dot(a, b):
    return jnp.dot(a, b, preferred_element_type=F32)


def _dot_nt(a, b):
    return lax.dot_general(a, b, (((1,), (1,)), ((), ())), preferred_element_type=F32)


def _dot01(x, m01):
    hi = x.astype(BF16)
    r1 = x - hi.astype(F32)
    mid = r1.astype(BF16)
    lo = (r1 - mid.astype(F32)).astype(BF16)
    return (_dot(lo, m01) + _dot(mid, m01)) + _dot(hi, m01)


def _cumsum_rows(x, n_rows):
    row = lax.broadcasted_iota(jnp.int32, x.shape, 0)
    s = 1
    while s < n_rows:
        x = x + jnp.where(row >= s, pltpu.roll(x, s, 0), 0.0)
        s *= 2
    return x


def _norm_matmul_kernel(x_ref, g_ref, w_ref, o_ref, xn_ref):
    @pl.when(pl.program_id(1) == 0)
    def _():
        xn_ref[...] = _rms(x_ref[...], g_ref[...]).astype(BF16)

    o_ref[...] = _dot(xn_ref[...], w_ref[...])


def norm_matmul(x, g, w, *, tm, tn):
    m, d = x.shape
    n = w.shape[1]
    return pl.pallas_call(
        _norm_matmul_kernel,
        grid=(m // tm, pl.cdiv(n, tn)),
        in_specs=[
            pl.BlockSpec((tm, d), lambda i, j: (i, 0)),
            pl.BlockSpec((1, d), lambda i, j: (0, 0)),
            pl.BlockSpec((d, tn), lambda i, j: (0, j)),
        ],
        out_specs=pl.BlockSpec((tm, tn), lambda i, j: (i, j)),
        out_shape=jax.ShapeDtypeStruct((m, n), F32),
        scratch_shapes=[pltpu.VMEM((tm, d), BF16)],
        compiler_params=_cparams(("parallel", "arbitrary")),
        name="norm_matmul",
    )(x, g.reshape(1, d), w)


def _matmul_norm_res_kernel(y_ref, w_ref, g_ref, h_ref, o_ref, acc_ref):
    k = pl.program_id(1)

    @pl.when(k == 0)
    def _():
        acc_ref[...] = jnp.zeros_like(acc_ref)

    acc_ref[...] += _dot(y_ref[...], w_ref[...])

    @pl.when(k == pl.num_programs(1) - 1)
    def _():
        o_ref[...] = h_ref[...] + _rms(acc_ref[...], g_ref[...])


def matmul_norm_res(y, w, g, h, *, tm, tk):
    m, kd = y.shape
    d = w.shape[1]
    return pl.pallas_call(
        _matmul_norm_res_kernel,
        grid=(m // tm, kd // tk),
        in_specs=[
            pl.BlockSpec((tm, tk), lambda i, k: (i, k)),
            pl.BlockSpec((tk, d), lambda i, k: (k, 0)),
            pl.BlockSpec((1, d), lambda i, k: (0, 0)),
            pl.BlockSpec((tm, d), lambda i, k: (i, 0)),
        ],
        out_specs=pl.BlockSpec((tm, d), lambda i, k: (i, 0)),
        out_shape=jax.ShapeDtypeStruct((m, d), F32),
        scratch_shapes=[pltpu.VMEM((tm, d), F32)],
        compiler_params=_cparams(("parallel", "arbitrary")),
        name="matmul_norm_res",
    )(y, w, g.reshape(1, d), h)


def _ffn_kernel(h_ref, gpre_ref, wg_ref, wu_ref, wd_ref, gpost_ref, o_ref, xn_ref, acc_ref):
    f = pl.program_id(1)

    @pl.when(f == 0)
    def _():
        xn_ref[...] = _rms(h_ref[...], gpre_ref[...]).astype(BF16)
        acc_ref[...] = jnp.zeros_like(acc_ref)

    xn = xn_ref[...]
    gate = _dot(xn, wg_ref[...])
    up = _dot(xn, wu_ref[...])
    act = (gate * _sigmoid(gate) * up).astype(BF16)
    acc_ref[...] += _dot(act, wd_ref[...])

    @pl.when(f == pl.num_programs(1) - 1)
    def _():
        o_ref[...] = h_ref[...] + _rms(acc_ref[...], gpost_ref[...])


def ffn(h, g_pre, w_gate, w_up, w_down, g_post, *, tm, tf):
    m, d = h.shape
    dff = w_gate.shape[1]
    return pl.pallas_call(
        _ffn_kernel,
        grid=(m // tm, dff // tf),
        in_specs=[
            pl.BlockSpec((tm, d), lambda i, f: (i, 0)),
            pl.BlockSpec((1, d), lambda i, f: (0, 0)),
            pl.BlockSpec((d, tf), lambda i, f: (0, f)),
            pl.BlockSpec((d, tf), lambda i, f: (0, f)),
            pl.BlockSpec((tf, d), lambda i, f: (f, 0)),
            pl.BlockSpec((1, d), lambda i, f: (0, 0)),
        ],
        out_specs=pl.BlockSpec((tm, d), lambda i, f: (i, 0)),
        out_shape=jax.ShapeDtypeStruct((m, d), F32),
        scratch_shapes=[pltpu.VMEM((tm, d), BF16), pltpu.VMEM((tm, d), F32)],
        compiler_params=_cparams(("parallel", "arbitrary")),
        name="ffn",
    )(h, g_pre.reshape(1, d), w_gate, w_up, w_down, g_post.reshape(1, d))


def _conv_silu(pad_ref, x, w_ref, b_ref, L):
    pad_ref[SUBLANES:SUBLANES + L, :] = x
    acc = b_ref[...] + x * w_ref[CONV_W - 1:CONV_W, :]
    for k in range(1, CONV_W):
        acc = acc + pad_ref[SUBLANES - k:SUBLANES - k + L, :] * w_ref[CONV_W - 1 - k:CONV_W - k, :]
    pad_ref[0:SUBLANES, :] = pad_ref[L:L + SUBLANES, :]
    return acc * _sigmoid(acc)


def _ssd_kernel(z_ref, x_ref, b_ref, c_ref, dt_ref,
                cwx_ref, cwb_ref, cwc_ref, cbx_ref, cbb_ref, cbc_ref,
                cix_ref, cib_ref, cic_ref,
                dtb_ref, alog_ref, dsk_ref, ng_ref, exp_ref, h0_ref,
                y_ref, hout_ref,
                xpad_ref, bpad_ref, cpad_ref, st_ref, *, L, valid_rows, n_heads):
    c = pl.program_id(2)
    gw = x_ref.shape[1]
    heads = gw // SSD_HEAD_DIM

    @pl.when(c == 0)
    def _():
        xpad_ref[0:SUBLANES, :] = cix_ref[...]
        bpad_ref[0:SUBLANES, :] = cib_ref[...]
        cpad_ref[0:SUBLANES, :] = cic_ref[...]
        st_ref[...] = h0_ref[...].reshape(gw, D_STATE).T

    xs = _conv_silu(xpad_ref, x_ref[...], cwx_ref, cbx_ref, L)
    bm = _conv_silu(bpad_ref, b_ref[...], cwb_ref, cbb_ref, L)
    cm = _conv_silu(cpad_ref, c_ref[...], cwc_ref, cbc_ref, L)

    lane = lax.broadcasted_iota(jnp.int32, (L, LANES), 1)
    row = lax.broadcasted_iota(jnp.int32, (L, LANES), 0)
    dt = _softplus(dt_ref[...] + dtb_ref[...])
    dt = jnp.where((lane < n_heads) & (row < valid_rows), dt, 0.0)
    dta = dt * (-jnp.exp(alog_ref[...]))
    acum = _cumsum_rows(dta, L)
    expand = exp_ref[...]
    dt_w = _dot01(dt, expand)
    acum_w = _dot01(acum, expand)
    a_last = acum_w[L - 1:L, :]

    xdt = xs * dt_w
    cm_bf = cm.astype(BF16)
    cb = _dot_nt(cm_bf, bm.astype(BF16))
    acum_t = acum_w.T
    li = lax.broadcasted_iota(jnp.int32, (L, L), 0)
    si = lax.broadcasted_iota(jnp.int32, (L, L), 1)
    tril = si <= li
    low_half = lax.broadcasted_iota(jnp.int32, (L, LANES), 1) < SSD_HEAD_DIM
    y_pairs = []
    for pr in range(heads // 2):
        xp = xdt[:, pr * LANES:(pr + 1) * LANES].astype(BF16)
        y_two = []
        for r in (2 * pr, 2 * pr + 1):
            col = acum_w[:, r * SSD_HEAD_DIM:r * SSD_HEAD_DIM + 1]
            rowv = acum_t[r * SSD_HEAD_DIM:r * SSD_HEAD_DIM + 1, :]
            decay = jnp.exp(jnp.where(tril, col - rowv, -jnp.inf))
            y_two.append(_dot((cb * decay).astype(BF16), xp))
        y_pairs.append(jnp.where(low_half, y_two[0], y_two[1]))
    y_diag = jnp.concatenate(y_pairs, axis=1)

    st = st_ref[...]
    y_off = jnp.exp(acum_w) * _dot(cm_bf, st.astype(BF16))
    xdec = (xdt * jnp.exp(a_last - acum_w)).astype(BF16)
    st_new = jnp.exp(a_last) * st + _dot(bm.T.astype(BF16), xdec)
    st_ref[...] = st_new

    y = y_diag + y_off + xs * dsk_ref[...]
    z = z_ref[...]
    y = y * (z * _sigmoid(z))
    y_ref[...] = _rms(y, ng_ref[...]).astype(y_ref.dtype)

    @pl.when(c == pl.num_programs(2) - 1)
    def _():
        hout_ref[...] = st_new.T.reshape(heads, SSD_HEAD_DIM, D_STATE)


def ssd_core(zx, conv_init, h0, layer, conv_w, conv_b, dt_bias, a_log, d_skip, norm_g,
             *, batch, seq, valid_rows):
    n_heads = dt_bias.shape[0]
    d_inner = n_heads * SSD_HEAD_DIM
    gw = d_inner // SSD_GROUPS
    hg = n_heads // SSD_GROUPS
    L = min(SSD_CHUNK, seq)
    nc = seq // L
    xb0 = d_inner // gw
    bb0 = (2 * d_inner) // D_STATE
    cb0 = bb0 + SSD_GROUPS
    dtb0 = (2 * d_inner + 2 * SSD_GROUPS * D_STATE) // LANES
    cwb0 = d_inner // D_STATE
    cwc0 = cwb0 + SSD_GROUPS

    pad_h = LANES - n_heads
    dtb = jnp.pad(dt_bias.astype(F32), (0, pad_h)).reshape(1, LANES)
    alog = jnp.pad(a_log.astype(F32), (0, pad_h)).reshape(1, LANES)
    dsk = jnp.repeat(d_skip.astype(F32), SSD_HEAD_DIM).reshape(1, d_inner)
    head_of_lane = jnp.arange(d_inner) // SSD_HEAD_DIM
    expand = (jnp.arange(LANES)[:, None] == head_of_lane[None, :]).astype(BF16)
    conv_b2 = conv_b.reshape(1, -1)

    row_blk = lambda b, g, c: b * nc + c
    kern = functools.partial(_ssd_kernel, L=L, valid_rows=valid_rows, n_heads=n_heads)
    y, hout = pl.pallas_call(
        kern,
        grid=(batch, SSD_GROUPS, nc),
        in_specs=[
            pl.BlockSpec((L, gw), lambda b, g, c: (row_blk(b, g, c), g)),
            pl.BlockSpec((L, gw), lambda b, g, c: (row_blk(b, g, c), xb0 + g)),
            pl.BlockSpec((L, D_STATE), lambda b, g, c: (row_blk(b, g, c), bb0 + g)),
            pl.BlockSpec((L, D_STATE), lambda b, g, c: (row_blk(b, g, c), cb0 + g)),
            pl.BlockSpec((L, LANES), lambda b, g, c: (row_blk(b, g, c), dtb0)),
            pl.BlockSpec((CONV_W, gw), lambda b, g, c: (0, g)),
            pl.BlockSpec((CONV_W, D_STATE), lambda b, g, c: (0, cwb0 + g)),
            pl.BlockSpec((CONV_W, D_STATE), lambda b, g, c: (0, cwc0 + g)),
            pl.BlockSpec((1, gw), lambda b, g, c: (0, g)),
            pl.BlockSpec((1, D_STATE), lambda b, g, c: (0, cwb0 + g)),
            pl.BlockSpec((1, D_STATE), lambda b, g, c: (0, cwc0 + g)),
            pl.BlockSpec((None, SUBLANES, gw), lambda b, g, c: (b, 0, g)),
            pl.BlockSpec((None, SUBLANES, D_STATE), lambda b, g, c: (b, 0, cwb0 + g)),
            pl.BlockSpec((None, SUBLANES, D_STATE), lambda b, g, c: (b, 0, cwc0 + g)),
            pl.BlockSpec((1, LANES), lambda b, g, c: (0, 0)),
            pl.BlockSpec((1, LANES), lambda b, g, c: (0, 0)),
            pl.BlockSpec((1, gw), lambda b, g, c: (0, g)),
            pl.BlockSpec((1, gw), lambda b, g, c: (0, g)),
            pl.BlockSpec((LANES, gw), lambda b, g, c: (0, g)),
            pl.BlockSpec((None, None, hg, SSD_HEAD_DIM, D_STATE), lambda b, g, c: (layer, b, g, 0, 0)),
        ],
        out_specs=[
            pl.BlockSpec((L, gw), lambda b, g, c: (row_blk(b, g, c), g)),
            pl.BlockSpec((None, hg, SSD_HEAD_DIM, D_STATE), lambda b, g, c: (b, g, 0, 0)),
        ],
        out_shape=[
            jax.ShapeDtypeStruct((batch * seq, d_inner), BF16),
            jax.ShapeDtypeStruct((batch, n_heads, SSD_HEAD_DIM, D_STATE), F32),
        ],
        scratch_shapes=[
            pltpu.VMEM((L + SUBLANES, gw), F32),
            pltpu.VMEM((L + SUBLANES, D_STATE), F32),
            pltpu.VMEM((L + SUBLANES, D_STATE), F32),
            pltpu.VMEM((D_STATE, gw), F32),
        ],
        compiler_params=_cparams(("parallel", "parallel", "arbitrary")),
        name="ssd_core",
    )(zx, zx, zx, zx, zx, conv_w, conv_w, conv_w, conv_b2, conv_b2, conv_b2,
      conv_init, conv_init, conv_init, dtb, alog, dsk, norm_g.reshape(1, d_inner), expand, h0)
    return y, hout


def _forget_kernel(p_ref, bf_ref, logf_ref, fcum_ref, *maybe_fcum_t_ref, seq, n_heads):
    lane = lax.broadcasted_iota(jnp.int32, (seq, LANES), 1)
    logit = p_ref[...] + bf_ref[...]
    logf = jnp.where(lane < n_heads, -_softplus(-logit), 0.0)
    fcum = _cumsum_rows(logf, seq)
    logf_ref[...] = logf[:, :n_heads]
    fcum_ref[...] = fcum[:, :n_heads]
    for fcum_t_ref in maybe_fcum_t_ref:
        fcum_t_ref[...] = fcum.T[:n_heads, :]


def fox_forget(proj, b_f, *, batch, seq, with_transposed):
    n_heads = b_f.shape[0]
    col_blk = (proj.shape[1] - n_heads) // LANES
    bfp = jnp.pad(b_f.astype(F32), (0, LANES - n_heads)).reshape(1, LANES)
    out_specs = [pl.BlockSpec((seq, n_heads), lambda b: (b, 0)),
                 pl.BlockSpec((seq, n_heads), lambda b: (b, 0))]
    out_shape = [jax.ShapeDtypeStruct((batch * seq, n_heads), F32),
                 jax.ShapeDtypeStruct((batch * seq, n_heads), F32)]
    if with_transposed:
        out_specs.append(pl.BlockSpec((None, n_heads, seq), lambda b: (b, 0, 0)))
        out_shape.append(jax.ShapeDtypeStruct((batch, n_heads, seq), F32))
    kern = functools.partial(_forget_kernel, seq=seq, n_heads=n_heads)
    return pl.pallas_call(
        kern,
        grid=(batch,),
        in_specs=[pl.BlockSpec((seq, LANES), lambda b: (b, col_blk)),
                  pl.BlockSpec((1, LANES), lambda b: (0, 0))],
        out_specs=out_specs,
        out_shape=out_shape,
        compiler_params=_cparams(("parallel",)),
        name="fox_forget",
    )(proj, bfp)


def _fox_flash_kernel(q_ref, k_ref, v_ref, f_ref, ft_ref, o_ref, m_ref, l_ref, acc_ref, fcol_ref, *, tq, tk):
    h = pl.program_id(1)
    qi = pl.program_id(2)
    ki = pl.program_id(3)
    scale = FOX_HEAD_DIM ** -0.5

    @pl.when(ki == 0)
    def _():
        m_ref[...] = jnp.full_like(m_ref, -jnp.inf)
        l_ref[...] = jnp.zeros_like(l_ref)
        acc_ref[...] = jnp.zeros_like(acc_ref)
        f = f_ref[...]
        hl = lax.broadcasted_iota(jnp.int32, f.shape, 1)
        fcol_ref[...] = jnp.sum(jnp.where(hl == h, f, 0.0), axis=1, keepdims=True)

    @pl.when(ki * tk < (qi + 1) * tq)
    def _():
        s = _dot_nt(q_ref[...].astype(BF16), k_ref[...].astype(BF16)) * scale
        s = s + fcol_ref[...] - ft_ref[pl.ds(h, 1), :]
        qpos = qi * tq + lax.broadcasted_iota(jnp.int32, (tq, tk), 0)
        kpos = ki * tk + lax.broadcasted_iota(jnp.int32, (tq, tk), 1)
        s = jnp.where(kpos <= qpos, s, -jnp.inf)
        m_old = m_ref[...]
        m_new = jnp.maximum(m_old, jnp.max(s, axis=1, keepdims=True))
        alpha = jnp.exp(m_old - m_new)
        p = jnp.exp(s - m_new)
        l_ref[...] = alpha * l_ref[...] + jnp.sum(p, axis=1, keepdims=True)
        acc_ref[...] = alpha * acc_ref[...] + _dot(p.astype(BF16), v_ref[...].astype(BF16))
        m_ref[...] = m_new

    @pl.when(ki == pl.num_programs(3) - 1)
    def _():
        o_ref[...] = (acc_ref[...] / l_ref[...]).astype(o_ref.dtype)


def fox_flash(proj, fcum, fcum_t, *, batch, seq, n_heads, tq, tk):
    d = n_heads * FOX_HEAD_DIM
    nq, nk = seq // tq, seq // tk
    last_k = lambda qi, ki: jnp.minimum(ki, ((qi + 1) * tq - 1) // tk)
    kern = functools.partial(_fox_flash_kernel, tq=tq, tk=tk)
    return pl.pallas_call(
        kern,
        grid=(batch, n_heads, nq, nk),
        in_specs=[
            pl.BlockSpec((tq, FOX_HEAD_DIM), lambda b, h, qi, ki: (b * nq + qi, h)),
            pl.BlockSpec((tk, FOX_HEAD_DIM), lambda b, h, qi, ki: (b * nk + last_k(qi, ki), n_heads + h)),
            pl.BlockSpec((tk, FOX_HEAD_DIM), lambda b, h, qi, ki: (b * nk + last_k(qi, ki), 2 * n_heads + h)),
            pl.BlockSpec((tq, n_heads), lambda b, h, qi, ki: (b * nq + qi, 0)),
            pl.BlockSpec((None, n_heads, tk), lambda b, h, qi, ki: (b, 0, last_k(qi, ki))),
        ],
        out_specs=pl.BlockSpec((tq, FOX_HEAD_DIM), lambda b, h, qi, ki: (b * nq + qi, h)),
        out_shape=jax.ShapeDtypeStruct((batch * seq, d), BF16),
        scratch_shapes=[pltpu.VMEM((tq, 1), F32), pltpu.VMEM((tq, 1), F32),
                        pltpu.VMEM((tq, FOX_HEAD_DIM), F32), pltpu.VMEM((tq, 1), F32)],
        compiler_params=_cparams(("parallel", "parallel", "parallel", "arbitrary")),
        name="fox_flash",
    )(proj, proj, proj, fcum, fcum_t)


def _past_bias_kernel(pt_ref, u_ref, *rest, pages_per_step):
    pp = pages_per_step
    lf_refs = rest[0:pp]
    o_ref = rest[pp]
    run_ref = rest[pp + 1]

    @pl.when(pl.program_id(1) == 0)
    def _():
        run_ref[...] = jnp.zeros_like(run_ref)

    for p in range(pp):
        lf = lf_refs[p][...]
        su = _dot01(lf, u_ref[...])
        run = run_ref[...]
        o_ref[pp - 1 - p] = run + su[:, :LANES]
        run_ref[...] = run + su[:, LANES:]


def fox_past_bias(cache_logf_t, page_table, layer, *, pages_per_step):
    batch, n_pages = page_table.shape
    n_heads = cache_logf_t.shape[2]
    pp = pages_per_step
    nblk = n_pages // pp
    pos = jnp.arange(PAGE_SIZE)
    u = jnp.concatenate([(pos[:, None] > pos[None, :]), jnp.ones((PAGE_SIZE, PAGE_SIZE), bool)],
                        axis=1).astype(BF16)

    def page_map(p):
        return lambda b, j, pt: (layer, pt[b, n_pages - 1 - (j * pp + p)], 0, 0)

    grid_spec = pltpu.PrefetchScalarGridSpec(
        num_scalar_prefetch=1,
        grid=(batch, nblk),
        in_specs=[pl.BlockSpec((PAGE_SIZE, 2 * PAGE_SIZE), lambda b, j, pt: (0, 0))]
        + [pl.BlockSpec((None, None, n_heads, PAGE_SIZE), page_map(p)) for p in range(pp)],
        out_specs=pl.BlockSpec((None, pp, n_heads, PAGE_SIZE), lambda b, j, pt: (b, nblk - 1 - j, 0, 0)),
        scratch_shapes=[pltpu.VMEM((n_heads, PAGE_SIZE), F32)],
    )
    return pl.pallas_call(
        functools.partial(_past_bias_kernel, pages_per_step=pp),
        grid_spec=grid_spec,
        out_shape=jax.ShapeDtypeStruct((batch, n_pages, n_heads, PAGE_SIZE), F32),
        compiler_params=_cparams(("parallel", "arbitrary")),
        name="fox_past_bias",
    )(page_table, u, *([cache_logf_t] * pp))


def _decode_kernel(pt_ref, q_ref, kn_ref, vn_ref, fcol_ref, frow_ref, trow_ref, *rest, n_heads, pages_per_step):
    pp = pages_per_step
    k_refs = rest[0:pp]
    v_refs = rest[pp:2 * pp]
    o_ref = rest[2 * pp]
    m_ref, l_ref, acc_ref, mb_ref, qb_ref = rest[2 * pp + 1:]
    j = pl.program_id(1)
    scale = FOX_HEAD_DIM ** -0.5
    rows = q_ref.shape[0]
    cols = k_refs[0].shape[0]

    @pl.when(j == 0)
    def _():
        r = lax.broadcasted_iota(jnp.int32, (rows, cols), 0)
        c = lax.broadcasted_iota(jnp.int32, (rows, cols), 1)
        same_head = (r % n_heads) == (c % n_heads)
        mb_ref[...] = jnp.where(same_head, fcol_ref[...], -jnp.inf)
        qb = q_ref[...].astype(BF16)
        qb_ref[...] = qb
        s = _dot_nt(qb, kn_ref[...].astype(BF16)) * scale + mb_ref[:, 0:rows] - frow_ref[...]
        rn = lax.broadcasted_iota(jnp.int32, (rows, rows), 0)
        cn = lax.broadcasted_iota(jnp.int32, (rows, rows), 1)
        s = jnp.where((cn // n_heads) <= (rn // n_heads), s, -jnp.inf)
        m_new = jnp.max(s, axis=1, keepdims=True)
        p = jnp.exp(s - m_new)
        m_ref[...] = m_new
        l_ref[...] = jnp.sum(p, axis=1, keepdims=True)
        acc_ref[...] = _dot(p.astype(BF16), vn_ref[...].astype(BF16))

    for pg in range(pp):
        s = _dot_nt(qb_ref[...], k_refs[pg][...].astype(BF16)) * scale + mb_ref[...] + trow_ref[pg]
        m_old = m_ref[...]
        m_new = jnp.maximum(m_old, jnp.max(s, axis=1, keepdims=True))
        alpha = jnp.exp(m_old - m_new)
        p = jnp.exp(s - m_new)
        l_ref[...] = alpha * l_ref[...] + jnp.sum(p, axis=1, keepdims=True)
        acc_ref[...] = alpha * acc_ref[...] + _dot(p.astype(BF16), v_refs[pg][...].astype(BF16))
        m_ref[...] = m_new

    @pl.when(j == pl.num_programs(1) - 1)
    def _():
        o_ref[...] = (acc_ref[...] / l_ref[...]).astype(o_ref.dtype)


def fox_decode(proj, fnew, past_bias, cache_k, cache_v, page_table, layer, *, batch, n_new, n_heads,
               pages_per_step):
    hd = FOX_HEAD_DIM
    d = n_heads * hd
    n_pages = page_table.shape[1]
    pp = pages_per_step
    rows = n_new * n_heads
    cols = PAGE_SIZE * n_heads
    q2 = proj[:, 0:d].reshape(batch, rows, hd)
    kn2 = proj[:, d:2 * d].reshape(batch, rows, hd)
    vn2 = proj[:, 2 * d:3 * d].reshape(batch, rows, hd)
    fcol = fnew.reshape(batch, rows, 1)
    frow = fnew.reshape(batch, 1, rows)
    trow = jnp.swapaxes(past_bias, 2, 3).reshape(batch, n_pages, 1, cols)

    def page_map(p):
        return lambda b, j, pt: (layer, pt[b, j * pp + p], 0, 0)

    in_specs = [
        pl.BlockSpec((None, rows, hd), lambda b, j, pt: (b, 0, 0)),
        pl.BlockSpec((None, rows, hd), lambda b, j, pt: (b, 0, 0)),
        pl.BlockSpec((None, rows, hd), lambda b, j, pt: (b, 0, 0)),
        pl.BlockSpec((None, rows, 1), lambda b, j, pt: (b, 0, 0)),
        pl.BlockSpec((None, 1, rows), lambda b, j, pt: (b, 0, 0)),
        pl.BlockSpec((None, pp, 1, cols), lambda b, j, pt: (b, j, 0, 0)),
    ]
    in_specs += [pl.BlockSpec((None, None, cols, hd), page_map(p)) for p in range(pp)]
    in_specs += [pl.BlockSpec((None, None, cols, hd), page_map(p)) for p in range(pp)]
    grid_spec = pltpu.PrefetchScalarGridSpec(
        num_scalar_prefetch=1,
        grid=(batch, n_pages // pp),
        in_specs=in_specs,
        out_specs=pl.BlockSpec((None, rows, hd), lambda b, j, pt: (b, 0, 0)),
        scratch_shapes=[
            pltpu.VMEM((rows, 1), F32),
            pltpu.VMEM((rows, 1), F32),
            pltpu.VMEM((rows, hd), F32),
            pltpu.VMEM((rows, cols), F32),
            pltpu.VMEM((rows, hd), BF16),
        ],
    )
    out = pl.pallas_call(
        functools.partial(_decode_kernel, n_heads=n_heads, pages_per_step=pp),
        grid_spec=grid_spec,
        out_shape=jax.ShapeDtypeStruct((batch, rows, hd), BF16),
        compiler_params=_cparams(("parallel", "arbitrary")),
        name="fox_decode",
    )(page_table, q2, kn2, vn2, fcol, frow, trow, *([cache_k] * pp), *([cache_v] * pp))
    return out.reshape(batch * n_new, d)


def kernel(x_prompt, x_sample, cache_k, cache_v, cache_logf, state_conv, state_ssm, page_table, ln_mix_pre, ln_mix_post, ln_ffn_pre, ln_ffn_post, ssd_w_in, ssd_conv_w, ssd_conv_b, ssd_dt_bias, ssd_a_log, ssd_d, ssd_norm, ssd_w_out, fox_w_in, fox_b_f, fox_w_out, ffn_w_gate, ffn_w_up, ffn_w_down):
    bp, lp, d = x_prompt.shape
    bs, ls, _ = x_sample.shape
    depth = ln_mix_pre.shape[0]
    n_att, n_pool, page, fox_heads, fox_hd = cache_k.shape
    ssd_heads = ssd_dt_bias.shape[1]
    d_inner = ssd_heads * SSD_HEAD_DIM
    conv_dim = ssd_conv_w.shape[2]
    mp, ms = bp * lp, bs * ls
    tmp, tms = min(mp, 512), min(ms, 512)

    hp = x_prompt.reshape(mp, d)
    hs = x_sample.reshape(ms, d)
    ck = cache_k.reshape(n_att, n_pool, page * fox_heads, fox_hd)
    cv = cache_v.reshape(n_att, n_pool, page * fox_heads, fox_hd)
    clf_t = jnp.swapaxes(cache_logf, 2, 3)

    zero_conv = jnp.zeros((bp, SUBLANES, conv_dim), F32)
    zero_ssm = jnp.zeros((1, bp, ssd_heads, SSD_HEAD_DIM, D_STATE), F32)
    conv_init_s = jnp.pad(state_conv, ((0, 0), (0, 0), (SUBLANES - (CONV_W - 1), 0), (0, 0)))
    ls_pad = SSD_CHUNK

    kp_l, vp_l, fp_l, cp_l, sp_l = [], [], [], [], []
    ks_l, vs_l, fs_l, cs_l, ss_l = [], [], [], [], []
    for i in range(depth):
        j = i // 2
        if i % 2 == 0:
            w_in = ssd_w_in[j].astype(BF16)
            w_out = ssd_w_out[j].astype(BF16)
            par = (ssd_conv_w[j], ssd_conv_b[j], ssd_dt_bias[j], ssd_a_log[j], ssd_d[j], ssd_norm[j])
            zx_p = norm_matmul(hp, ln_mix_pre[i], w_in, tm=tmp, tn=512)
            zx_s = norm_matmul(hs, ln_mix_pre[i], w_in, tm=tms, tn=512)
            y_p, st_p = ssd_core(zx_p, zero_conv, zero_ssm, 0, *par, batch=bp, seq=lp, valid_rows=SSD_CHUNK)
            zx_s3 = zx_s.reshape(bs, ls, -1)
            zx_sp = jnp.pad(zx_s3, ((0, 0), (0, ls_pad - ls), (0, 0))).reshape(bs * ls_pad, -1)
            y_sp, st_s = ssd_core(zx_sp, conv_init_s[j], state_ssm, j, *par, batch=bs, seq=ls_pad, valid_rows=ls)
            y_s = y_sp.reshape(bs, ls_pad, d_inner)[:, :ls].reshape(ms, d_inner)
            hp = matmul_norm_res(y_p, w_out, ln_mix_post[i], hp, tm=tmp, tk=512)
            hs = matmul_norm_res(y_s, w_out, ln_mix_post[i], hs, tm=tms, tk=512)
            cp_l.append(zx_p.reshape(bp, lp, -1)[:, lp - (CONV_W - 1):, d_inner:d_inner + conv_dim])
            cs_l.append(zx_s3[:, ls - (CONV_W - 1):, d_inner:d_inner + conv_dim])
            sp_l.append(st_p)
            ss_l.append(st_s)
        else:
            w_in = fox_w_in[j].astype(BF16)
            w_out = fox_w_out[j].astype(BF16)
            pr_p = norm_matmul(hp, ln_mix_pre[i], w_in, tm=tmp, tn=512)
            pr_s = norm_matmul(hs, ln_mix_pre[i], w_in, tm=tms, tn=512)
            lf_p, fc_p, fct_p = fox_forget(pr_p, fox_b_f[j], batch=bp, seq=lp, with_transposed=True)
            lf_s, fc_s = fox_forget(pr_s, fox_b_f[j], batch=bs, seq=ls, with_transposed=False)
            a_p = fox_flash(pr_p, fc_p, fct_p, batch=bp, seq=lp, n_heads=fox_heads, tq=512, tk=512)
            tb_s = fox_past_bias(clf_t, page_table, j, pages_per_step=8)
            a_s = fox_decode(pr_s, fc_s, tb_s, ck, cv, page_table, j, batch=bs, n_new=ls,
                             n_heads=fox_heads, pages_per_step=4)
            hp = matmul_norm_res(a_p, w_out, ln_mix_post[i], hp, tm=tmp, tk=512)
            hs = matmul_norm_res(a_s, w_out, ln_mix_post[i], hs, tm=tms, tk=512)
            kp_l.append(pr_p[:, d:2 * d].reshape(bp, lp, fox_heads, fox_hd))
            vp_l.append(pr_p[:, 2 * d:3 * d].reshape(bp, lp, fox_heads, fox_hd))
            fp_l.append(lf_p.reshape(bp, lp, fox_heads))
            ks_l.append(pr_s[:, d:2 * d].reshape(bs, ls, fox_heads, fox_hd))
            vs_l.append(pr_s[:, 2 * d:3 * d].reshape(bs, ls, fox_heads, fox_hd))
            fs_l.append(lf_s.reshape(bs, ls, fox_heads))
        wg = ffn_w_gate[i].astype(BF16)
        wu = ffn_w_up[i].astype(BF16)
        wd = ffn_w_down[i].astype(BF16)
        hp = ffn(hp, ln_ffn_pre[i], wg, wu, wd, ln_ffn_post[i], tm=tmp, tf=512)
        hs = ffn(hs, ln_ffn_pre[i], wg, wu, wd, ln_ffn_post[i], tm=tms, tf=512)
    return (hp.reshape(bp, lp, d), hs.reshape(bs, ls, d),
            jnp.stack(kp_l), jnp.stack(vp_l), jnp.stack(fp_l), jnp.stack(cp_l), jnp.stack(sp_l),
            jnp.stack(ks_l), jnp.stack(vs_l), jnp.stack(fs_l), jnp.stack(cs_l), jnp.stack(ss_l))
```

```python
import functools

import jax
import jax.numpy as jnp
from jax import lax
from jax.experimental import pallas as pl
from jax.experimental.pallas import tpu as pltpu

F32 = jnp.float32
BF16 = jnp.bfloat16

RMS_EPS = 1e-6
LOG2E = 1.4426950408889634
LANES = 128
SUBLANES = 8
SSD_HEAD_DIM = 64
SSD_GROUPS = 8
D_STATE = 128
CONV_W = 4
SSD_CHUNK = 128
FOX_HEAD_DIM = 128
PAGE_SIZE = 128
VMEM_LIMIT = 56 * 1024 * 1024
SSD_IN_TN = 1152
FOX_IN_TN = 896
SSD_GROUPS_PER_STEP = 4


def _cparams(sem):
    return pltpu.CompilerParams(dimension_semantics=sem, vmem_limit_bytes=VMEM_LIMIT)


def _sigmoid(x):
    return 1.0 / (1.0 + jnp.exp(-x))


def _softplus(x):
    return jnp.maximum(x, 0.0) + jnp.log1p(jnp.exp(-jnp.abs(x)))


def _rms(x, g):
    ms = jnp.mean(x * x, axis=-1, keepdims=True)
    return x * lax.rsqrt(ms + RMS_EPS) * g


def _dot(a, b):
    return jnp.dot(a, b, preferred_element_type=F32)


def _dot_nt(a, b):
    return lax.dot_general(a, b, (((1,), (1,)), ((), ())), preferred_element_type=F32)


def _split3(x):
    hi = x.astype(BF16)
    r1 = x - hi.astype(F32)
    mid = r1.astype(BF16)
    lo = (r1 - mid.astype(F32)).astype(BF16)
    return hi, mid, lo


def _dot01(x, m01):
    hi, mid, lo = _split3(x)
    return (_dot(lo, m01) + _dot(mid, m01)) + _dot(hi, m01)


def _cumsum_rows(x, n_rows):
    row = lax.broadcasted_iota(jnp.int32, x.shape, 0)
    s = 1
    while s < n_rows:
        x = x + jnp.where(row >= s, pltpu.roll(x, s, 0), 0.0)
        s *= 2
    return x


def _norm_matmul_kernel(x_ref, g_ref, w_ref, o_ref, xn_ref):
    @pl.when(pl.program_id(1) == 0)
    def _():
        xn_ref[...] = _rms(x_ref[...], g_ref[...]).astype(BF16)

    o_ref[...] = _dot(xn_ref[...], w_ref[...])


def norm_matmul(x, g, w, layer, *, tm, tn):
    m, d = x.shape
    n = w.shape[2]
    return pl.pallas_call(
        _norm_matmul_kernel,
        grid=(m // tm, pl.cdiv(n, tn)),
        in_specs=[
            pl.BlockSpec((tm, d), lambda i, j: (i, 0)),
            pl.BlockSpec((1, d), lambda i, j: (0, 0)),
            pl.BlockSpec((None, d, tn), lambda i, j: (layer, 0, j)),
        ],
        out_specs=pl.BlockSpec((tm, tn), lambda i, j: (i, j)),
        out_shape=jax.ShapeDtypeStruct((m, n), F32),
        scratch_shapes=[pltpu.VMEM((tm, d), BF16)],
        compiler_params=_cparams(("parallel", "arbitrary")),
        name="norm_matmul",
    )(x, g.reshape(1, d), w)


def _matmul_norm_res_kernel(y_ref, w_ref, g_ref, h_ref, o_ref):
    o_ref[...] = h_ref[...] + _rms(_dot(y_ref[...], w_ref[...]), g_ref[...])


def matmul_norm_res(y, w, layer, g, h, *, tm):
    m, kd = y.shape
    d = w.shape[2]
    return pl.pallas_call(
        _matmul_norm_res_kernel,
        grid=(m // tm,),
        in_specs=[
            pl.BlockSpec((tm, kd), lambda i: (i, 0)),
            pl.BlockSpec((None, kd, d), lambda i: (layer, 0, 0)),
            pl.BlockSpec((1, d), lambda i: (0, 0)),
            pl.BlockSpec((tm, d), lambda i: (i, 0)),
        ],
        out_specs=pl.BlockSpec((tm, d), lambda i: (i, 0)),
        out_shape=jax.ShapeDtypeStruct((m, d), F32),
        compiler_params=_cparams(("parallel",)),
        name="matmul_norm_res",
    )(y, w, g.reshape(1, d), h)


def _ffn_kernel(h_ref, gpre_ref, wg_ref, wu_ref, wd_ref, gpost_ref, o_ref, xn_ref, acc_ref):
    f = pl.program_id(1)

    @pl.when(f == 0)
    def _():
        xn_ref[...] = _rms(h_ref[...], gpre_ref[...]).astype(BF16)
        acc_ref[...] = jnp.zeros_like(acc_ref)

    xn = xn_ref[...]
    gate = _dot(xn, wg_ref[...])
    up = _dot(xn, wu_ref[...])
    act = (gate * _sigmoid(gate) * up).astype(BF16)
    acc_ref[...] += _dot(act, wd_ref[...])

    @pl.when(f == pl.num_programs(1) - 1)
    def _():
        o_ref[...] = h_ref[...] + _rms(acc_ref[...], gpost_ref[...])


def ffn(h, g_pre, w_gate, w_up, w_down, layer, g_post, *, tm, tf):
    m, d = h.shape
    dff = w_gate.shape[2]
    return pl.pallas_call(
        _ffn_kernel,
        grid=(m // tm, dff // tf),
        in_specs=[
            pl.BlockSpec((tm, d), lambda i, f: (i, 0)),
            pl.BlockSpec((1, d), lambda i, f: (0, 0)),
            pl.BlockSpec((None, d, tf), lambda i, f: (layer, 0, f)),
            pl.BlockSpec((None, d, tf), lambda i, f: (layer, 0, f)),
            pl.BlockSpec((None, tf, d), lambda i, f: (layer, f, 0)),
            pl.BlockSpec((1, d), lambda i, f: (0, 0)),
        ],
        out_specs=pl.BlockSpec((tm, d), lambda i, f: (i, 0)),
        out_shape=jax.ShapeDtypeStruct((m, d), F32),
        scratch_shapes=[pltpu.VMEM((tm, d), BF16), pltpu.VMEM((tm, d), F32)],
        compiler_params=_cparams(("parallel", "arbitrary")),
        name="ffn",
    )(h, g_pre.reshape(1, d), w_gate, w_up, w_down, g_post.reshape(1, d))


def _conv_silu(pad_ref, x, w_ref, b_ref, L):
    pad_ref[SUBLANES:SUBLANES + L, :] = x
    acc = b_ref[...] + x * w_ref[CONV_W - 1:CONV_W, :]
    for k in range(1, CONV_W):
        acc = acc + pad_ref[SUBLANES - k:SUBLANES - k + L, :] * w_ref[CONV_W - 1 - k:CONV_W - k, :]
    pad_ref[0:SUBLANES, :] = pad_ref[L:L + SUBLANES, :]
    return acc * _sigmoid(acc)


def _ssd_kernel(z_ref, x_ref, b_ref, c_ref, dt_ref,
                cwx_ref, cwb_ref, cwc_ref, cbx_ref, cbb_ref, cbc_ref,
                cix_ref, cib_ref, cic_ref,
                dtb_ref, alog_ref, dsk_ref, ng_ref, exp_ref, h0_ref,
                y_ref, hout_ref,
                xpad_ref, bpad_ref, cpad_ref, st_ref, *, L, valid_rows, n_heads, gw):
    c = pl.program_id(2)
    width = x_ref.shape[1]
    n_grp = width // gw
    heads = gw // SSD_HEAD_DIM

    @pl.when(c == 0)
    def _():
        xpad_ref[0:SUBLANES, :] = cix_ref[...]
        bpad_ref[0:SUBLANES, :] = cib_ref[...]
        cpad_ref[0:SUBLANES, :] = cic_ref[...]
        st_ref[...] = h0_ref[...].reshape(width, D_STATE).T

    xs = _conv_silu(xpad_ref, x_ref[...], cwx_ref, cbx_ref, L)
    bm = _conv_silu(bpad_ref, b_ref[...], cwb_ref, cbb_ref, L)
    cm = _conv_silu(cpad_ref, c_ref[...], cwc_ref, cbc_ref, L)

    lane = lax.broadcasted_iota(jnp.int32, (L, LANES), 1)
    row = lax.broadcasted_iota(jnp.int32, (L, LANES), 0)
    dt = _softplus(dt_ref[...] + dtb_ref[...])
    dt = jnp.where((lane < n_heads) & (row < valid_rows), dt, 0.0)
    dta = dt * (-jnp.exp(alog_ref[...]))
    acum = _cumsum_rows(dta, L)
    expand = exp_ref[...]
    dt_w = _dot01(dt, expand)
    acum_w = _dot01(acum, expand)
    a_last = acum_w[L - 1:L, :]

    xdt = xs * dt_w
    xdec = (xdt * jnp.exp(a_last - acum_w)).astype(BF16)
    exp_acum = jnp.exp(acum_w)
    exp_last = jnp.exp(a_last)
    acum_t = acum_w.T
    li = lax.broadcasted_iota(jnp.int32, (L, L), 0)
    si = lax.broadcasted_iota(jnp.int32, (L, L), 1)
    tril = si <= li
    low_half = lax.broadcasted_iota(jnp.int32, (L, LANES), 1) < SSD_HEAD_DIM
    z = z_ref[...]
    gate = z * _sigmoid(z)
    for gi in range(n_grp):
        g0 = gi * gw
        n0 = gi * D_STATE
        bm_g = bm[:, n0:n0 + D_STATE]
        cm_bf = cm[:, n0:n0 + D_STATE].astype(BF16)
        cb = _dot_nt(cm_bf, bm_g.astype(BF16))
        y_pairs = []
        for pr in range(heads // 2):
            p0 = g0 + pr * LANES
            xp = xdt[:, p0:p0 + LANES].astype(BF16)
            y_two = []
            for r0 in (p0, p0 + SSD_HEAD_DIM):
                col = acum_w[:, r0:r0 + 1]
                rowv = acum_t[r0:r0 + 1, :]
                decay = jnp.exp(jnp.where(tril, col - rowv, -jnp.inf))
                y_two.append(_dot((cb * decay).astype(BF16), xp))
            y_pairs.append(jnp.where(low_half, y_two[0], y_two[1]))
        y_diag = jnp.concatenate(y_pairs, axis=1)

        st = st_ref[:, g0:g0 + gw]
        y_off = exp_acum[:, g0:g0 + gw] * _dot(cm_bf, st.astype(BF16))
        st_new = exp_last[:, g0:g0 + gw] * st + _dot(bm_g.T.astype(BF16), xdec[:, g0:g0 + gw])
        st_ref[:, g0:g0 + gw] = st_new

        y = y_diag + y_off + xs[:, g0:g0 + gw] * dsk_ref[:, g0:g0 + gw]
        y = y * gate[:, g0:g0 + gw]
        y_ref[:, g0:g0 + gw] = _rms(y, ng_ref[:, g0:g0 + gw]).astype(y_ref.dtype)

    @pl.when(c == pl.num_programs(2) - 1)
    def _():
        hout_ref[...] = st_ref[...].T.reshape(n_grp * heads, SSD_HEAD_DIM, D_STATE)


def ssd_core(zx, conv_init, h0, layer, conv_w, conv_b, dt_bias, a_log, d_skip, norm_g,
             *, batch, seq, valid_rows, groups_per_step):
    n_heads = dt_bias.shape[0]
    d_inner = n_heads * SSD_HEAD_DIM
    gw = d_inner // SSD_GROUPS
    gps = groups_per_step
    wx = gps * gw
    wn = gps * D_STATE
    hs = gps * (n_heads // SSD_GROUPS)
    L = min(SSD_CHUNK, seq)
    nc = seq // L
    xb0 = d_inner // wx
    bb0 = (2 * d_inner) // wn
    cb0 = (2 * d_inner + SSD_GROUPS * D_STATE) // wn
    dtb0 = (2 * d_inner + 2 * SSD_GROUPS * D_STATE) // LANES
    cwb0 = d_inner // wn
    cwc0 = (d_inner + SSD_GROUPS * D_STATE) // wn

    pad_h = LANES - n_heads
    dtb = jnp.pad(dt_bias.astype(F32), (0, pad_h)).reshape(1, LANES)
    alog = jnp.pad(a_log.astype(F32), (0, pad_h)).reshape(1, LANES)
    dsk = jnp.repeat(d_skip.astype(F32), SSD_HEAD_DIM).reshape(1, d_inner)
    head_of_lane = jnp.arange(d_inner) // SSD_HEAD_DIM
    expand = (jnp.arange(LANES)[:, None] == head_of_lane[None, :]).astype(BF16)
    conv_b2 = conv_b.reshape(1, -1)

    row_blk = lambda b, g, c: b * nc + c
    kern = functools.partial(_ssd_kernel, L=L, valid_rows=valid_rows, n_heads=n_heads, gw=gw)
    y, hout = pl.pallas_call(
        kern,
        grid=(batch, SSD_GROUPS // gps, nc),
        in_specs=[
            pl.BlockSpec((L, wx), lambda b, g, c: (row_blk(b, g, c), g)),
            pl.BlockSpec((L, wx), lambda b, g, c: (row_blk(b, g, c), xb0 + g)),
            pl.BlockSpec((L, wn), lambda b, g, c: (row_blk(b, g, c), bb0 + g)),
            pl.BlockSpec((L, wn), lambda b, g, c: (row_blk(b, g, c), cb0 + g)),
            pl.BlockSpec((L, LANES), lambda b, g, c: (row_blk(b, g, c), dtb0)),
            pl.BlockSpec((CONV_W, wx), lambda b, g, c: (0, g)),
            pl.BlockSpec((CONV_W, wn), lambda b, g, c: (0, cwb0 + g)),
            pl.BlockSpec((CONV_W, wn), lambda b, g, c: (0, cwc0 + g)),
            pl.BlockSpec((1, wx), lambda b, g, c: (0, g)),
            pl.BlockSpec((1, wn), lambda b, g, c: (0, cwb0 + g)),
            pl.BlockSpec((1, wn), lambda b, g, c: (0, cwc0 + g)),
            pl.BlockSpec((None, SUBLANES, wx), lambda b, g, c: (b, 0, g)),
            pl.BlockSpec((None, SUBLANES, wn), lambda b, g, c: (b, 0, cwb0 + g)),
            pl.BlockSpec((None, SUBLANES, wn), lambda b, g, c: (b, 0, cwc0 + g)),
            pl.BlockSpec((1, LANES), lambda b, g, c: (0, 0)),
            pl.BlockSpec((1, LANES), lambda b, g, c: (0, 0)),
            pl.BlockSpec((1, wx), lambda b, g, c: (0, g)),
            pl.BlockSpec((1, wx), lambda b, g, c: (0, g)),
            pl.BlockSpec((LANES, wx), lambda b, g, c: (0, g)),
            pl.BlockSpec((None, None, hs, SSD_HEAD_DIM, D_STATE), lambda b, g, c: (layer, b, g, 0, 0)),
        ],
        out_specs=[
            pl.BlockSpec((L, wx), lambda b, g, c: (row_blk(b, g, c), g)),
            pl.BlockSpec((None, hs, SSD_HEAD_DIM, D_STATE), lambda b, g, c: (b, g, 0, 0)),
        ],
        out_shape=[
            jax.ShapeDtypeStruct((batch * seq, d_inner), BF16),
            jax.ShapeDtypeStruct((batch, n_heads, SSD_HEAD_DIM, D_STATE), F32),
        ],
        scratch_shapes=[
            pltpu.VMEM((L + SUBLANES, wx), F32),
            pltpu.VMEM((L + SUBLANES, wn), F32),
            pltpu.VMEM((L + SUBLANES, wn), F32),
            pltpu.VMEM((D_STATE, wx), F32),
        ],
        compiler_params=_cparams(("parallel", "parallel", "arbitrary")),
        name="ssd_core",
    )(zx, zx, zx, zx, zx, conv_w, conv_w, conv_w, conv_b2, conv_b2, conv_b2,
      conv_init, conv_init, conv_init, dtb, alog, dsk, norm_g.reshape(1, d_inner), expand, h0)
    return y, hout


def _forget_kernel(p_ref, bf_ref, logf_ref, fcum_ref, *, seq, n_heads):
    lane = lax.broadcasted_iota(jnp.int32, (seq, LANES), 1)
    logit = p_ref[...] + bf_ref[...]
    logf = jnp.where(lane < n_heads, -_softplus(-logit), 0.0)
    logf_ref[...] = logf[:, :n_heads]
    fcum_ref[...] = _cumsum_rows(logf, seq)


def fox_forget(proj, b_f, *, batch, seq):
    n_heads = b_f.shape[0]
    col_blk = (proj.shape[1] - n_heads) // LANES
    bfp = jnp.pad(b_f.astype(F32), (0, LANES - n_heads)).reshape(1, LANES)
    return pl.pallas_call(
        functools.partial(_forget_kernel, seq=seq, n_heads=n_heads),
        grid=(batch,),
        in_specs=[pl.BlockSpec((seq, LANES), lambda b: (b, col_blk)),
                  pl.BlockSpec((1, LANES), lambda b: (0, 0))],
        out_specs=[pl.BlockSpec((seq, n_heads), lambda b: (b, 0)),
                   pl.BlockSpec((seq, LANES), lambda b: (b, 0))],
        out_shape=[jax.ShapeDtypeStruct((batch * seq, n_heads), F32),
                   jax.ShapeDtypeStruct((batch * seq, LANES), F32)],
        compiler_params=_cparams(("parallel",)),
        name="fox_forget",
    )(proj, bfp)


N_PIECES = 3


def _fox_flash_kernel(q_ref, k_ref, v_ref, f_ref, o_ref, qa_ref, ka_ref, vb_ref, *, seq, tq):
    h = pl.program_id(1)
    hd = FOX_HEAD_DIM
    c1 = LOG2E * hd ** -0.5
    pieces = jnp.concatenate(_split3(f_ref[...] * hd ** 0.5), axis=1)
    r = lax.broadcasted_iota(jnp.int32, (N_PIECES * LANES, LANES), 0)
    c = lax.broadcasted_iota(jnp.int32, (N_PIECES * LANES, LANES), 1)
    is_h = (r % LANES) == h
    sel_q = jnp.where(is_h & (c == r // LANES), 1.0, 0.0).astype(BF16)
    sel_k = jnp.where(is_h & (c == N_PIECES + r // LANES), -1.0, 0.0).astype(BF16)
    lane = lax.broadcasted_iota(jnp.int32, (seq, LANES), 1)
    ones_q = jnp.where((lane >= N_PIECES) & (lane < 2 * N_PIECES), 1.0, 0.0)
    ones_k = jnp.where(lane < N_PIECES, 1.0, 0.0)
    qa_ref[:, 0:hd] = q_ref[...].astype(BF16)
    qa_ref[:, hd:2 * hd] = (_dot(pieces, sel_q) + ones_q).astype(BF16)
    ka_ref[:, 0:hd] = k_ref[...].astype(BF16)
    ka_ref[:, hd:2 * hd] = (_dot(pieces, sel_k) + ones_k).astype(BF16)
    vb_ref[...] = v_ref[...].astype(BF16)

    ti = lax.broadcasted_iota(jnp.int32, (tq, tq), 0)
    si = lax.broadcasted_iota(jnp.int32, (tq, tq), 1)
    for n in range(seq // tq):
        r0, r1 = n * tq, (n + 1) * tq
        qa = qa_ref[r0:r1, :]
        u_d = jnp.where(si <= ti, _dot_nt(qa, ka_ref[r0:r1, :]), -jnp.inf)
        m = jnp.max(u_d, axis=1, keepdims=True)
        if n > 0:
            u_o = _dot_nt(qa, ka_ref[0:r0, :])
            m = jnp.maximum(m, jnp.max(u_o, axis=1, keepdims=True))
        p_d = jnp.exp2((u_d - m) * c1)
        l = jnp.sum(p_d, axis=1, keepdims=True)
        acc = _dot(p_d.astype(BF16), vb_ref[r0:r1, :])
        if n > 0:
            p_o = jnp.exp2((u_o - m) * c1)
            l = l + jnp.sum(p_o, axis=1, keepdims=True)
            acc = acc + _dot(p_o.astype(BF16), vb_ref[0:r0, :])
        o_ref[r0:r1, :] = (acc / l).astype(o_ref.dtype)


def fox_flash(proj, fcum, *, batch, seq, n_heads, tq):
    hd = FOX_HEAD_DIM
    kern = functools.partial(_fox_flash_kernel, seq=seq, tq=tq)
    return pl.pallas_call(
        kern,
        grid=(batch, n_heads),
        in_specs=[
            pl.BlockSpec((seq, hd), lambda b, h: (b, h)),
            pl.BlockSpec((seq, hd), lambda b, h: (b, n_heads + h)),
            pl.BlockSpec((seq, hd), lambda b, h: (b, 2 * n_heads + h)),
            pl.BlockSpec((seq, LANES), lambda b, h: (b, 0)),
        ],
        out_specs=pl.BlockSpec((seq, hd), lambda b, h: (b, h)),
        out_shape=jax.ShapeDtypeStruct((batch * seq, n_heads * hd), BF16),
        scratch_shapes=[pltpu.VMEM((seq, 2 * hd), BF16), pltpu.VMEM((seq, 2 * hd), BF16),
                        pltpu.VMEM((seq, hd), BF16)],
        compiler_params=_cparams(("parallel", "parallel")),
        name="fox_flash",
    )(proj, proj, proj, fcum)


def _past_bias_kernel(pt_ref, u_ref, *rest, pages_per_step):
    pp = pages_per_step
    lf_refs = rest[0:pp]
    o_ref = rest[pp]
    run_ref = rest[pp + 1]

    @pl.when(pl.program_id(1) == 0)
    def _():
        run_ref[...] = jnp.zeros_like(run_ref)

    for p in range(pp):
        lf = lf_refs[p][...]
        su = _dot01(lf, u_ref[...])
        run = run_ref[...]
        o_ref[pp - 1 - p] = run + su[:, :LANES]
        run_ref[...] = run + su[:, LANES:]


def fox_past_bias(cache_logf_t, page_table, layer, *, pages_per_step):
    batch, n_pages = page_table.shape
    n_heads = cache_logf_t.shape[2]
    pp = pages_per_step
    nblk = n_pages // pp
    pos = jnp.arange(PAGE_SIZE)
    u = jnp.concatenate([(pos[:, None] > pos[None, :]), jnp.ones((PAGE_SIZE, PAGE_SIZE), bool)],
                        axis=1).astype(BF16)

    def page_map(p):
        return lambda b, j, pt: (layer, pt[b, n_pages - 1 - (j * pp + p)], 0, 0)

    grid_spec = pltpu.PrefetchScalarGridSpec(
        num_scalar_prefetch=1,
        grid=(batch, nblk),
        in_specs=[pl.BlockSpec((PAGE_SIZE, 2 * PAGE_SIZE), lambda b, j, pt: (0, 0))]
        + [pl.BlockSpec((None, None, n_heads, PAGE_SIZE), page_map(p)) for p in range(pp)],
        out_specs=pl.BlockSpec((None, pp, n_heads, PAGE_SIZE), lambda b, j, pt: (b, nblk - 1 - j, 0, 0)),
        scratch_shapes=[pltpu.VMEM((n_heads, PAGE_SIZE), F32)],
    )
    return pl.pallas_call(
        functools.partial(_past_bias_kernel, pages_per_step=pp),
        grid_spec=grid_spec,
        out_shape=jax.ShapeDtypeStruct((batch, n_pages, n_heads, PAGE_SIZE), F32),
        compiler_params=_cparams(("parallel", "arbitrary")),
        name="fox_past_bias",
    )(page_table, u, *([cache_logf_t] * pp))


def _decode_kernel(pt_ref, q_ref, kn_ref, vn_ref, fcol_ref, frow_ref, trow_ref, *rest, n_heads, pages_per_step):
    pp = pages_per_step
    k_refs = rest[0:pp]
    v_refs = rest[pp:2 * pp]
    o_ref = rest[2 * pp]
    m_ref, l_ref, acc_ref, mb_ref, qb_ref = rest[2 * pp + 1:]
    j = pl.program_id(1)
    c1 = LOG2E * FOX_HEAD_DIM ** -0.5
    bias_scale = FOX_HEAD_DIM ** 0.5
    rows = q_ref.shape[0]
    cols = k_refs[0].shape[0]

    @pl.when(j == 0)
    def _():
        r = lax.broadcasted_iota(jnp.int32, (rows, cols), 0)
        c = lax.broadcasted_iota(jnp.int32, (rows, cols), 1)
        same_head = (r % n_heads) == (c % n_heads)
        mb_ref[...] = jnp.where(same_head, fcol_ref[...] * bias_scale, -jnp.inf)
        qb = q_ref[...].astype(BF16)
        qb_ref[...] = qb
        s = _dot_nt(qb, kn_ref[...].astype(BF16)) + mb_ref[:, 0:rows] - frow_ref[...] * bias_scale
        rn = lax.broadcasted_iota(jnp.int32, (rows, rows), 0)
        cn = lax.broadcasted_iota(jnp.int32, (rows, rows), 1)
        s = jnp.where((cn // n_heads) <= (rn // n_heads), s, -jnp.inf)
        m_new = jnp.max(s, axis=1, keepdims=True)
        p = jnp.exp2((s - m_new) * c1)
        m_ref[...] = m_new
        l_ref[...] = jnp.sum(p, axis=1, keepdims=True)
        acc_ref[...] = _dot(p.astype(BF16), vn_ref[...].astype(BF16))

    s_pages = [_dot_nt(qb_ref[...], k_refs[pg][...].astype(BF16)) + mb_ref[...] + trow_ref[pg] * bias_scale
               for pg in range(pp)]
    m_old = m_ref[...]
    m_new = m_old
    for s in s_pages:
        m_new = jnp.maximum(m_new, jnp.max(s, axis=1, keepdims=True))
    alpha = jnp.exp2((m_old - m_new) * c1)
    l_new = alpha * l_ref[...]
    acc_new = alpha * acc_ref[...]
    for pg, s in enumerate(s_pages):
        p = jnp.exp2((s - m_new) * c1)
        l_new = l_new + jnp.sum(p, axis=1, keepdims=True)
        acc_new = acc_new + _dot(p.astype(BF16), v_refs[pg][...].astype(BF16))
    l_ref[...] = l_new
    acc_ref[...] = acc_new
    m_ref[...] = m_new

    @pl.when(j == pl.num_programs(1) - 1)
    def _():
        o_ref[...] = (acc_ref[...] / l_ref[...]).astype(o_ref.dtype)


def fox_decode(proj, fnew, past_bias, cache_k, cache_v, page_table, layer, *, batch, n_new, n_heads,
               pages_per_step):
    hd = FOX_HEAD_DIM
    d = n_heads * hd
    n_pages = page_table.shape[1]
    pp = pages_per_step
    rows = n_new * n_heads
    cols = PAGE_SIZE * n_heads
    q2 = proj[:, 0:d].reshape(batch, rows, hd)
    kn2 = proj[:, d:2 * d].reshape(batch, rows, hd)
    vn2 = proj[:, 2 * d:3 * d].reshape(batch, rows, hd)
    fcol = fnew.reshape(batch, rows, 1)
    frow = fnew.reshape(batch, 1, rows)
    trow = jnp.swapaxes(past_bias, 2, 3).reshape(batch, n_pages, 1, cols)

    def page_map(p):
        return lambda b, j, pt: (layer, pt[b, j * pp + p], 0, 0)

    in_specs = [
        pl.BlockSpec((None, rows, hd), lambda b, j, pt: (b, 0, 0)),
        pl.BlockSpec((None, rows, hd), lambda b, j, pt: (b, 0, 0)),
        pl.BlockSpec((None, rows, hd), lambda b, j, pt: (b, 0, 0)),
        pl.BlockSpec((None, rows, 1), lambda b, j, pt: (b, 0, 0)),
        pl.BlockSpec((None, 1, rows), lambda b, j, pt: (b, 0, 0)),
        pl.BlockSpec((None, pp, 1, cols), lambda b, j, pt: (b, j, 0, 0)),
    ]
    in_specs += [pl.BlockSpec((None, None, cols, hd), page_map(p)) for p in range(pp)]
    in_specs += [pl.BlockSpec((None, None, cols, hd), page_map(p)) for p in range(pp)]
    grid_spec = pltpu.PrefetchScalarGridSpec(
        num_scalar_prefetch=1,
        grid=(batch, n_pages // pp),
        in_specs=in_specs,
        out_specs=pl.BlockSpec((None, rows, hd), lambda b, j, pt: (b, 0, 0)),
        scratch_shapes=[
            pltpu.VMEM((rows, 1), F32),
            pltpu.VMEM((rows, 1), F32),
            pltpu.VMEM((rows, hd), F32),
            pltpu.VMEM((rows, cols), F32),
            pltpu.VMEM((rows, hd), BF16),
        ],
    )
    out = pl.pallas_call(
        functools.partial(_decode_kernel, n_heads=n_heads, pages_per_step=pp),
        grid_spec=grid_spec,
        out_shape=jax.ShapeDtypeStruct((batch, rows, hd), BF16),
        compiler_params=_cparams(("parallel", "arbitrary")),
        name="fox_decode",
    )(page_table, q2, kn2, vn2, fcol, frow, trow, *([cache_k] * pp), *([cache_v] * pp))
    return out.reshape(batch * n_new, d)


def kernel(x_prompt, x_sample, cache_k, cache_v, cache_logf, state_conv, state_ssm, page_table, ln_mix_pre, ln_mix_post, ln_ffn_pre, ln_ffn_post, ssd_w_in, ssd_conv_w, ssd_conv_b, ssd_dt_bias, ssd_a_log, ssd_d, ssd_norm, ssd_w_out, fox_w_in, fox_b_f, fox_w_out, ffn_w_gate, ffn_w_up, ffn_w_down):
    bp, lp, d = x_prompt.shape
    bs, ls, _ = x_sample.shape
    depth = ln_mix_pre.shape[0]
    n_att, n_pool, page, fox_heads, fox_hd = cache_k.shape
    ssd_heads = ssd_dt_bias.shape[1]
    d_inner = ssd_heads * SSD_HEAD_DIM
    conv_dim = ssd_conv_w.shape[2]
    mp, ms = bp * lp, bs * ls
    tmp, tms = min(mp, 1024), min(ms, 1024)

    hp = x_prompt.reshape(mp, d)
    hs = x_sample.reshape(ms, d)
    ck = cache_k.reshape(n_att, n_pool, page * fox_heads, fox_hd)
    cv = cache_v.reshape(n_att, n_pool, page * fox_heads, fox_hd)
    clf_t = jnp.swapaxes(cache_logf, 2, 3)

    zero_conv = jnp.zeros((bp, SUBLANES, conv_dim), F32)
    zero_ssm = jnp.zeros((1, bp, ssd_heads, SSD_HEAD_DIM, D_STATE), F32)
    conv_init_s = jnp.pad(state_conv, ((0, 0), (0, 0), (SUBLANES - (CONV_W - 1), 0), (0, 0)))
    ls_pad = SSD_CHUNK

    w_ssd_in, w_ssd_out = ssd_w_in.astype(BF16), ssd_w_out.astype(BF16)
    w_fox_in, w_fox_out = fox_w_in.astype(BF16), fox_w_out.astype(BF16)
    w_gate, w_up, w_down = ffn_w_gate.astype(BF16), ffn_w_up.astype(BF16), ffn_w_down.astype(BF16)

    kp_l, vp_l, fp_l, cp_l, sp_l = [], [], [], [], []
    ks_l, vs_l, fs_l, cs_l, ss_l = [], [], [], [], []
    for i in range(depth):
        j = i // 2
        if i % 2 == 0:
            par = (ssd_conv_w[j], ssd_conv_b[j], ssd_dt_bias[j], ssd_a_log[j], ssd_d[j], ssd_norm[j])
            zx_p = norm_matmul(hp, ln_mix_pre[i], w_ssd_in, j, tm=tmp, tn=SSD_IN_TN)
            zx_s = norm_matmul(hs, ln_mix_pre[i], w_ssd_in, j, tm=tms, tn=SSD_IN_TN)
            y_p, st_p = ssd_core(zx_p, zero_conv, zero_ssm, 0, *par, batch=bp, seq=lp, valid_rows=SSD_CHUNK,
                                 groups_per_step=SSD_GROUPS_PER_STEP)
            zx_s3 = zx_s.reshape(bs, ls, -1)
            zx_sp = jnp.pad(zx_s3, ((0, 0), (0, ls_pad - ls), (0, 0))).reshape(bs * ls_pad, -1)
            y_sp, st_s = ssd_core(zx_sp, conv_init_s[j], state_ssm, j, *par, batch=bs, seq=ls_pad, valid_rows=ls,
                                  groups_per_step=SSD_GROUPS_PER_STEP)
            y_s = y_sp.reshape(bs, ls_pad, d_inner)[:, :ls].reshape(ms, d_inner)
            hp = matmul_norm_res(y_p, w_ssd_out, j, ln_mix_post[i], hp, tm=min(mp, 256))
            hs = matmul_norm_res(y_s, w_ssd_out, j, ln_mix_post[i], hs, tm=tms)
            cp_l.append(zx_p.reshape(bp, lp, -1)[:, lp - (CONV_W - 1):, d_inner:d_inner + conv_dim])
            cs_l.append(zx_s3[:, ls - (CONV_W - 1):, d_inner:d_inner + conv_dim])
            sp_l.append(st_p)
            ss_l.append(st_s)
        else:
            pr_p = norm_matmul(hp, ln_mix_pre[i], w_fox_in, j, tm=tmp, tn=FOX_IN_TN)
            pr_s = norm_matmul(hs, ln_mix_pre[i], w_fox_in, j, tm=tms, tn=FOX_IN_TN)
            lf_p, fc_p = fox_forget(pr_p, fox_b_f[j], batch=bp, seq=lp)
            lf_s, fc_s = fox_forget(pr_s, fox_b_f[j], batch=bs, seq=ls)
            a_p = fox_flash(pr_p, fc_p, batch=bp, seq=lp, n_heads=fox_heads, tq=512)
            tb_s = fox_past_bias(clf_t, page_table, j, pages_per_step=8)
            a_s = fox_decode(pr_s, fc_s[:, :fox_heads], tb_s, ck, cv, page_table, j, batch=bs, n_new=ls,
                             n_heads=fox_heads, pages_per_step=4)
            hp = matmul_norm_res(a_p, w_fox_out, j, ln_mix_post[i], hp, tm=min(mp, 512))
            hs = matmul_norm_res(a_s, w_fox_out, j, ln_mix_post[i], hs, tm=tms)
            kp_l.append(pr_p[:, d:2 * d].reshape(bp, lp, fox_heads, fox_hd))
            vp_l.append(pr_p[:, 2 * d:3 * d].reshape(bp, lp, fox_heads, fox_hd))
            fp_l.append(lf_p.reshape(bp, lp, fox_heads))
            ks_l.append(pr_s[:, d:2 * d].reshape(bs, ls, fox_heads, fox_hd))
            vs_l.append(pr_s[:, 2 * d:3 * d].reshape(bs, ls, fox_heads, fox_hd))
            fs_l.append(lf_s.reshape(bs, ls, fox_heads))
        hp = ffn(hp, ln_ffn_pre[i], w_gate, w_up, w_down, i, ln_ffn_post[i], tm=min(mp, 512), tf=512)
        hs = ffn(hs, ln_ffn_pre[i], w_gate, w_up, w_down, i, ln_ffn_post[i], tm=tms, tf=512)
    return (hp.reshape(bp, lp, d), hs.reshape(bs, ls, d),
            jnp.stack(kp_l), jnp.stack(vp_l), jnp.stack(fp_l), jnp.stack(cp_l), jnp.stack(sp_l),
            jnp.stack(ks_l), jnp.stack(vs_l), jnp.stack(fs_l), jnp.stack(cs_l), jnp.stack(ss_l))
```

```python
import functools

import jax
import jax.numpy as jnp
from jax import lax
from jax.experimental import pallas as pl
from jax.experimental.pallas import tpu as pltpu

F32 = jnp.float32
BF16 = jnp.bfloat16

RMS_EPS = 1e-6
LOG2E = 1.4426950408889634
LANES = 128
SUBLANES = 8
SSD_HEAD_DIM = 64
SSD_GROUPS = 8
D_STATE = 128
CONV_W = 4
SSD_CHUNK = 128
FOX_HEAD_DIM = 128
PAGE_SIZE = 128
VMEM_LIMIT = 56 * 1024 * 1024
SSD_IN_TN = 1152
FOX_IN_TN = 1024
SSD_GROUPS_PER_STEP = 4


def _cparams(sem):
    return pltpu.CompilerParams(dimension_semantics=sem, vmem_limit_bytes=VMEM_LIMIT)


def _sigmoid(x):
    return 1.0 / (1.0 + jnp.exp(-x))


def _softplus(x):
    return jnp.maximum(x, 0.0) + jnp.log1p(jnp.exp(-jnp.abs(x)))


def _rms(x, g):
    ms = jnp.mean(x * x, axis=-1, keepdims=True)
    return x * lax.rsqrt(ms + RMS_EPS) * g


def _dot(a, b):
    return jnp.dot(a, b, preferred_element_type=F32)


def _dot_nt(a, b):
    return lax.dot_general(a, b, (((1,), (1,)), ((), ())), preferred_element_type=F32)


def _split3(x):
    hi = x.astype(BF16)
    r1 = x - hi.astype(F32)
    mid = r1.astype(BF16)
    lo = (r1 - mid.astype(F32)).astype(BF16)
    return hi, mid, lo


def _dot01(x, m01):
    hi, mid, lo = _split3(x)
    return (_dot(lo, m01) + _dot(mid, m01)) + _dot(hi, m01)


def _cumsum_rows(x, n_rows):
    row = lax.broadcasted_iota(jnp.int32, x.shape, 0)
    s = 1
    while s < n_rows:
        x = x + jnp.where(row >= s, pltpu.roll(x, s, 0), 0.0)
        s *= 2
    return x


def _norm_matmul_kernel(x_ref, g_ref, w_ref, o_ref, *rest):
    *maybe_wb_ref, xn_ref = rest

    @pl.when(pl.program_id(1) == 0)
    def _():
        xn_ref[...] = _rms(x_ref[...], g_ref[...]).astype(BF16)

    wb = w_ref[...].astype(BF16)
    for wb_ref in maybe_wb_ref:
        wb_ref[...] = wb
    o_ref[...] = _dot(xn_ref[...], wb)


def norm_matmul(x, g, w, layer, *, tm, tn):
    m, d = x.shape
    n = w.shape[2]
    emit_wb = w.dtype != BF16
    assert not emit_wb or m == tm
    out_specs = [pl.BlockSpec((tm, tn), lambda i, j: (i, j))]
    out_shape = [jax.ShapeDtypeStruct((m, n), F32)]
    if emit_wb:
        out_specs.append(pl.BlockSpec((None, d, tn), lambda i, j: (0, 0, j)))
        out_shape.append(jax.ShapeDtypeStruct((1, d, n), BF16))
    res = pl.pallas_call(
        _norm_matmul_kernel,
        grid=(m // tm, pl.cdiv(n, tn)),
        in_specs=[
            pl.BlockSpec((tm, d), lambda i, j: (i, 0)),
            pl.BlockSpec((1, d), lambda i, j: (0, 0)),
            pl.BlockSpec((None, d, tn), lambda i, j: (layer, 0, j)),
        ],
        out_specs=out_specs,
        out_shape=out_shape,
        scratch_shapes=[pltpu.VMEM((tm, d), BF16)],
        compiler_params=_cparams(("parallel", "arbitrary")),
        name="norm_matmul",
    )(x, g.reshape(1, d), w)
    return res if emit_wb else res[0]


def _fox_in_proj_kernel(x_ref, g_ref, w_ref, wf_ref, *rest, tiles_per_part, emit_wb, aliased):
    rest = rest[2:] if aliased else rest
    q_ref, k_ref, v_ref, f_ref = rest[:4]
    xn_ref = rest[-1]
    j = pl.program_id(1)
    wb = w_ref[...].astype(BF16)
    if emit_wb:
        wb_ref, wfb_ref = rest[4:6]
        wb_ref[...] = wb

    @pl.when(j == 0)
    def _():
        xn = _rms(x_ref[...], g_ref[...]).astype(BF16)
        xn_ref[...] = xn
        wfb = wf_ref[...].astype(BF16)
        if emit_wb:
            wfb_ref[...] = wfb
        f_ref[...] = _dot(xn, wfb)

    y = _dot(xn_ref[...], wb)

    @pl.when(j < tiles_per_part)
    def _():
        q_ref[...] = y

    @pl.when((j >= tiles_per_part) & (j < 2 * tiles_per_part))
    def _():
        k_ref[...] = y

    @pl.when(j >= 2 * tiles_per_part)
    def _():
        v_ref[...] = y


def fox_in_proj(x, g, w, w_f, layer, k_stack, v_stack, slot, n_slots, *, tm, tn, n_heads):
    m, d = x.shape
    n = w.shape[2]
    tpp = d // tn
    f_blk = (n - n_heads) // LANES
    emit_wb = w.dtype != BF16
    aliased = k_stack is not None
    assert not emit_wb or m == tm
    part = lambda p: (lambda i, j: (i, jnp.clip(j - p * tpp, 0, tpp - 1)))
    spart = lambda p: (lambda i, j: (slot, i, jnp.clip(j - p * tpp, 0, tpp - 1)))
    in_specs = [
        pl.BlockSpec((tm, d), lambda i, j: (i, 0)),
        pl.BlockSpec((1, d), lambda i, j: (0, 0)),
        pl.BlockSpec((None, d, tn), lambda i, j: (layer, 0, j)),
        pl.BlockSpec((None, d, LANES), lambda i, j: (layer, 0, f_blk)),
    ]
    args = [x, g.reshape(1, d), w, w_f]
    aliases = {}
    if aliased:
        in_specs += [pl.BlockSpec(memory_space=pl.ANY), pl.BlockSpec(memory_space=pl.ANY)]
        args += [k_stack, v_stack]
        aliases = {4: 1, 5: 2}
    out_specs = [
        pl.BlockSpec((tm, tn), part(0)),
        pl.BlockSpec((None, tm, tn), spart(1)),
        pl.BlockSpec((None, tm, tn), spart(2)),
        pl.BlockSpec((tm, LANES), lambda i, j: (i, 0)),
    ]
    out_shape = [
        jax.ShapeDtypeStruct((m, d), F32),
        jax.ShapeDtypeStruct((n_slots, m, d), F32),
        jax.ShapeDtypeStruct((n_slots, m, d), F32),
        jax.ShapeDtypeStruct((m, LANES), F32),
    ]
    if emit_wb:
        out_specs += [pl.BlockSpec((None, d, tn), lambda i, j: (0, 0, j)),
                      pl.BlockSpec((None, d, LANES), lambda i, j: (0, 0, f_blk))]
        out_shape += [jax.ShapeDtypeStruct((1, d, n), BF16), jax.ShapeDtypeStruct((1, d, n), BF16)]
    res = pl.pallas_call(
        functools.partial(_fox_in_proj_kernel, tiles_per_part=tpp, emit_wb=emit_wb, aliased=aliased),
        grid=(m // tm, 3 * tpp),
        in_specs=in_specs,
        out_specs=out_specs,
        out_shape=out_shape,
        input_output_aliases=aliases,
        scratch_shapes=[pltpu.VMEM((tm, d), BF16)],
        compiler_params=_cparams(("arbitrary", "arbitrary")),
        name="fox_in_proj",
    )(*args)
    return res


def _matmul_norm_res_kernel(y_ref, w_ref, g_ref, h_ref, o_ref):
    o_ref[...] = h_ref[...] + _rms(_dot(y_ref[...], w_ref[...]), g_ref[...])


def matmul_norm_res(y, w, layer, g, h, *, tm):
    m, kd = y.shape
    d = w.shape[2]
    return pl.pallas_call(
        _matmul_norm_res_kernel,
        grid=(m // tm,),
        in_specs=[
            pl.BlockSpec((tm, kd), lambda i: (i, 0)),
            pl.BlockSpec((None, kd, d), lambda i: (layer, 0, 0)),
            pl.BlockSpec((1, d), lambda i: (0, 0)),
            pl.BlockSpec((tm, d), lambda i: (i, 0)),
        ],
        out_specs=pl.BlockSpec((tm, d), lambda i: (i, 0)),
        out_shape=jax.ShapeDtypeStruct((m, d), F32),
        compiler_params=_cparams(("parallel",)),
        name="matmul_norm_res",
    )(y, w, g.reshape(1, d), h)


def _ffn_kernel(h_ref, gpre_ref, wg_ref, wu_ref, wd_ref, gpost_ref, o_ref, *rest):
    *maybe_wb_refs, xn_ref, acc_ref = rest
    f = pl.program_id(1)

    @pl.when(f == 0)
    def _():
        xn_ref[...] = _rms(h_ref[...], gpre_ref[...]).astype(BF16)
        acc_ref[...] = jnp.zeros_like(acc_ref)

    wg, wu, wd = (r[...].astype(BF16) for r in (wg_ref, wu_ref, wd_ref))
    for wb_ref, wb in zip(maybe_wb_refs, (wg, wu, wd)):
        wb_ref[...] = wb
    xn = xn_ref[...]
    gate = _dot(xn, wg)
    up = _dot(xn, wu)
    act = (gate * _sigmoid(gate) * up).astype(BF16)
    acc_ref[...] += _dot(act, wd)

    @pl.when(f == pl.num_programs(1) - 1)
    def _():
        o_ref[...] = h_ref[...] + _rms(acc_ref[...], gpost_ref[...])


def ffn(h, g_pre, w_gate, w_up, w_down, layer, g_post, *, tm, tf):
    m, d = h.shape
    dff = w_gate.shape[2]
    emit_wb = w_gate.dtype != BF16
    assert not emit_wb or m == tm
    out_specs = [pl.BlockSpec((tm, d), lambda i, f: (i, 0))]
    out_shape = [jax.ShapeDtypeStruct((m, d), F32)]
    if emit_wb:
        out_specs += [pl.BlockSpec((None, d, tf), lambda i, f: (0, 0, f)),
                      pl.BlockSpec((None, d, tf), lambda i, f: (0, 0, f)),
                      pl.BlockSpec((None, tf, d), lambda i, f: (0, f, 0))]
        out_shape += [jax.ShapeDtypeStruct((1, d, dff), BF16), jax.ShapeDtypeStruct((1, d, dff), BF16),
                      jax.ShapeDtypeStruct((1, dff, d), BF16)]
    res = pl.pallas_call(
        _ffn_kernel,
        grid=(m // tm, dff // tf),
        in_specs=[
            pl.BlockSpec((tm, d), lambda i, f: (i, 0)),
            pl.BlockSpec((1, d), lambda i, f: (0, 0)),
            pl.BlockSpec((None, d, tf), lambda i, f: (layer, 0, f)),
            pl.BlockSpec((None, d, tf), lambda i, f: (layer, 0, f)),
            pl.BlockSpec((None, tf, d), lambda i, f: (layer, f, 0)),
            pl.BlockSpec((1, d), lambda i, f: (0, 0)),
        ],
        out_specs=out_specs,
        out_shape=out_shape,
        scratch_shapes=[pltpu.VMEM((tm, d), BF16), pltpu.VMEM((tm, d), F32)],
        compiler_params=_cparams(("parallel", "arbitrary")),
        name="ffn",
    )(h, g_pre.reshape(1, d), w_gate, w_up, w_down, g_post.reshape(1, d))
    return res if emit_wb else res[0]


def _conv_silu(pad_ref, x, w_ref, b_ref, L):
    pad_ref[SUBLANES:SUBLANES + L, :] = x
    acc = b_ref[...] + x * w_ref[CONV_W - 1:CONV_W, :]
    for k in range(1, CONV_W):
        acc = acc + pad_ref[SUBLANES - k:SUBLANES - k + L, :] * w_ref[CONV_W - 1 - k:CONV_W - k, :]
    pad_ref[0:SUBLANES, :] = pad_ref[L:L + SUBLANES, :]
    return acc * _sigmoid(acc)


def _ssd_kernel(z_ref, x_ref, b_ref, c_ref, dt_ref,
                cwx_ref, cwb_ref, cwc_ref, cbx_ref, cbb_ref, cbc_ref,
                cix_ref, cib_ref, cic_ref,
                dtb_ref, alog_ref, dsk_ref, ng_ref, exp_ref, h0_ref,
                y_ref, hout_ref,
                xpad_ref, bpad_ref, cpad_ref, st_ref, *, L, valid_rows, n_heads, gw):
    c = pl.program_id(2)
    width = x_ref.shape[1]
    n_grp = width // gw
    heads = gw // SSD_HEAD_DIM

    @pl.when(c == 0)
    def _():
        xpad_ref[0:SUBLANES, :] = cix_ref[...]
        bpad_ref[0:SUBLANES, :] = cib_ref[...]
        cpad_ref[0:SUBLANES, :] = cic_ref[...]
        st_ref[...] = h0_ref[...].reshape(width, D_STATE).T

    xs = _conv_silu(xpad_ref, x_ref[...], cwx_ref, cbx_ref, L)
    bm = _conv_silu(bpad_ref, b_ref[...], cwb_ref, cbb_ref, L)
    cm = _conv_silu(cpad_ref, c_ref[...], cwc_ref, cbc_ref, L)

    lane = lax.broadcasted_iota(jnp.int32, (L, LANES), 1)
    row = lax.broadcasted_iota(jnp.int32, (L, LANES), 0)
    dt = _softplus(dt_ref[...] + dtb_ref[...])
    dt = jnp.where((lane < n_heads) & (row < valid_rows), dt, 0.0)
    dta = dt * (-jnp.exp(alog_ref[...]))
    acum = _cumsum_rows(dta, L)
    expand = exp_ref[...]
    dt_w = _dot01(dt, expand)
    acum_w = _dot01(acum, expand)
    a_last = acum_w[L - 1:L, :]

    xdt = xs * dt_w
    xdec = (xdt * jnp.exp(a_last - acum_w)).astype(BF16)
    exp_acum = jnp.exp(acum_w)
    exp_last = jnp.exp(a_last)
    acum_t = acum_w.T
    li = lax.broadcasted_iota(jnp.int32, (L, L), 0)
    si = lax.broadcasted_iota(jnp.int32, (L, L), 1)
    tril = si <= li
    low_half = lax.broadcasted_iota(jnp.int32, (L, LANES), 1) < SSD_HEAD_DIM
    z = z_ref[...]
    gate = z * _sigmoid(z)
    for gi in range(n_grp):
        g0 = gi * gw
        n0 = gi * D_STATE
        bm_g = bm[:, n0:n0 + D_STATE]
        cm_bf = cm[:, n0:n0 + D_STATE].astype(BF16)
        cb = _dot_nt(cm_bf, bm_g.astype(BF16))
        y_pairs = []
        for pr in range(heads // 2):
            p0 = g0 + pr * LANES
            xp = xdt[:, p0:p0 + LANES].astype(BF16)
            y_two = []
            for r0 in (p0, p0 + SSD_HEAD_DIM):
                col = acum_w[:, r0:r0 + 1]
                rowv = acum_t[r0:r0 + 1, :]
                decay = jnp.exp(jnp.where(tril, col - rowv, -jnp.inf))
                y_two.append(_dot((cb * decay).astype(BF16), xp))
            y_pairs.append(jnp.where(low_half, y_two[0], y_two[1]))
        y_diag = jnp.concatenate(y_pairs, axis=1)

        st = st_ref[:, g0:g0 + gw]
        y_off = exp_acum[:, g0:g0 + gw] * _dot(cm_bf, st.astype(BF16))
        st_new = exp_last[:, g0:g0 + gw] * st + _dot(bm_g.T.astype(BF16), xdec[:, g0:g0 + gw])
        st_ref[:, g0:g0 + gw] = st_new

        y = y_diag + y_off + xs[:, g0:g0 + gw] * dsk_ref[:, g0:g0 + gw]
        y = y * gate[:, g0:g0 + gw]
        y_ref[:, g0:g0 + gw] = _rms(y, ng_ref[:, g0:g0 + gw]).astype(y_ref.dtype)

    @pl.when(c == pl.num_programs(2) - 1)
    def _():
        hout_ref[...] = st_ref[...].T.reshape(n_grp * heads, SSD_HEAD_DIM, D_STATE)


def ssd_core(zx, conv_init, h0, layer, conv_w, conv_b, dt_bias, a_log, d_skip, norm_g,
             *, batch, seq, valid_rows, groups_per_step):
    n_heads = dt_bias.shape[0]
    d_inner = n_heads * SSD_HEAD_DIM
    gw = d_inner // SSD_GROUPS
    gps = groups_per_step
    wx = gps * gw
    wn = gps * D_STATE
    hs = gps * (n_heads // SSD_GROUPS)
    L = min(SSD_CHUNK, seq)
    nc = seq // L
    xb0 = d_inner // wx
    bb0 = (2 * d_inner) // wn
    cb0 = (2 * d_inner + SSD_GROUPS * D_STATE) // wn
    dtb0 = (2 * d_inner + 2 * SSD_GROUPS * D_STATE) // LANES
    cwb0 = d_inner // wn
    cwc0 = (d_inner + SSD_GROUPS * D_STATE) // wn

    pad_h = LANES - n_heads
    dtb = jnp.pad(dt_bias.astype(F32), (0, pad_h)).reshape(1, LANES)
    alog = jnp.pad(a_log.astype(F32), (0, pad_h)).reshape(1, LANES)
    dsk = jnp.repeat(d_skip.astype(F32), SSD_HEAD_DIM).reshape(1, d_inner)
    head_of_lane = jnp.arange(d_inner) // SSD_HEAD_DIM
    expand = (jnp.arange(LANES)[:, None] == head_of_lane[None, :]).astype(BF16)
    conv_b2 = conv_b.reshape(1, -1)

    row_blk = lambda b, g, c: b * nc + c
    kern = functools.partial(_ssd_kernel, L=L, valid_rows=valid_rows, n_heads=n_heads, gw=gw)
    y, hout = pl.pallas_call(
        kern,
        grid=(batch, SSD_GROUPS // gps, nc),
        in_specs=[
            pl.BlockSpec((L, wx), lambda b, g, c: (row_blk(b, g, c), g)),
            pl.BlockSpec((L, wx), lambda b, g, c: (row_blk(b, g, c), xb0 + g)),
            pl.BlockSpec((L, wn), lambda b, g, c: (row_blk(b, g, c), bb0 + g)),
            pl.BlockSpec((L, wn), lambda b, g, c: (row_blk(b, g, c), cb0 + g)),
            pl.BlockSpec((L, LANES), lambda b, g, c: (row_blk(b, g, c), dtb0)),
            pl.BlockSpec((CONV_W, wx), lambda b, g, c: (0, g)),
            pl.BlockSpec((CONV_W, wn), lambda b, g, c: (0, cwb0 + g)),
            pl.BlockSpec((CONV_W, wn), lambda b, g, c: (0, cwc0 + g)),
            pl.BlockSpec((1, wx), lambda b, g, c: (0, g)),
            pl.BlockSpec((1, wn), lambda b, g, c: (0, cwb0 + g)),
            pl.BlockSpec((1, wn), lambda b, g, c: (0, cwc0 + g)),
            pl.BlockSpec((None, SUBLANES, wx), lambda b, g, c: (b, 0, g)),
            pl.BlockSpec((None, SUBLANES, wn), lambda b, g, c: (b, 0, cwb0 + g)),
            pl.BlockSpec((None, SUBLANES, wn), lambda b, g, c: (b, 0, cwc0 + g)),
            pl.BlockSpec((1, LANES), lambda b, g, c: (0, 0)),
            pl.BlockSpec((1, LANES), lambda b, g, c: (0, 0)),
            pl.BlockSpec((1, wx), lambda b, g, c: (0, g)),
            pl.BlockSpec((1, wx), lambda b, g, c: (0, g)),
            pl.BlockSpec((LANES, wx), lambda b, g, c: (0, g)),
            pl.BlockSpec((None, None, hs, SSD_HEAD_DIM, D_STATE), lambda b, g, c: (layer, b, g, 0, 0)),
        ],
        out_specs=[
            pl.BlockSpec((L, wx), lambda b, g, c: (row_blk(b, g, c), g)),
            pl.BlockSpec((None, hs, SSD_HEAD_DIM, D_STATE), lambda b, g, c: (b, g, 0, 0)),
        ],
        out_shape=[
            jax.ShapeDtypeStruct((batch * seq, d_inner), BF16),
            jax.ShapeDtypeStruct((batch, n_heads, SSD_HEAD_DIM, D_STATE), F32),
        ],
        scratch_shapes=[
            pltpu.VMEM((L + SUBLANES, wx), F32),
            pltpu.VMEM((L + SUBLANES, wn), F32),
            pltpu.VMEM((L + SUBLANES, wn), F32),
            pltpu.VMEM((D_STATE, wx), F32),
        ],
        compiler_params=_cparams(("parallel", "parallel", "arbitrary")),
        name="ssd_core",
    )(zx, zx, zx, zx, zx, conv_w, conv_w, conv_w, conv_b2, conv_b2, conv_b2,
      conv_init, conv_init, conv_init, dtb, alog, dsk, norm_g.reshape(1, d_inner), expand, h0)
    return y, hout


def _forget_kernel(p_ref, bf_ref, logf_ref, fcum_ref, *, seq, n_heads):
    lane = lax.broadcasted_iota(jnp.int32, (seq, LANES), 1)
    logit = p_ref[...] + bf_ref[...]
    logf = jnp.where(lane < n_heads, -_softplus(-logit), 0.0)
    logf_ref[...] = logf[:, :n_heads]
    fcum_ref[...] = _cumsum_rows(logf, seq)


def fox_forget(flog, b_f, *, batch, seq):
    n_heads = b_f.shape[0]
    bfp = jnp.pad(b_f.astype(F32), (0, LANES - n_heads)).reshape(1, LANES)
    return pl.pallas_call(
        functools.partial(_forget_kernel, seq=seq, n_heads=n_heads),
        grid=(batch,),
        in_specs=[pl.BlockSpec((seq, LANES), lambda b: (b, 0)),
                  pl.BlockSpec((1, LANES), lambda b: (0, 0))],
        out_specs=[pl.BlockSpec((seq, n_heads), lambda b: (b, 0)),
                   pl.BlockSpec((seq, LANES), lambda b: (b, 0))],
        out_shape=[jax.ShapeDtypeStruct((batch * seq, n_heads), F32),
                   jax.ShapeDtypeStruct((batch * seq, LANES), F32)],
        compiler_params=_cparams(("parallel",)),
        name="fox_forget",
    )(flog, bfp)


N_PIECES = 3


def _fox_flash_kernel(q_ref, k_ref, v_ref, f_ref, o_ref, qa_ref, ka_ref, vb_ref, *, seq, tq):
    h = pl.program_id(1)
    hd = FOX_HEAD_DIM
    c1 = LOG2E * hd ** -0.5
    pieces = jnp.concatenate(_split3(f_ref[...] * hd ** 0.5), axis=1)
    r = lax.broadcasted_iota(jnp.int32, (N_PIECES * LANES, LANES), 0)
    c = lax.broadcasted_iota(jnp.int32, (N_PIECES * LANES, LANES), 1)
    is_h = (r % LANES) == h
    sel_q = jnp.where(is_h & (c == r // LANES), 1.0, 0.0).astype(BF16)
    sel_k = jnp.where(is_h & (c == N_PIECES + r // LANES), -1.0, 0.0).astype(BF16)
    lane = lax.broadcasted_iota(jnp.int32, (seq, LANES), 1)
    ones_q = jnp.where((lane >= N_PIECES) & (lane < 2 * N_PIECES), 1.0, 0.0)
    ones_k = jnp.where(lane < N_PIECES, 1.0, 0.0)
    qa_ref[:, 0:hd] = q_ref[...].astype(BF16)
    qa_ref[:, hd:2 * hd] = (_dot(pieces, sel_q) + ones_q).astype(BF16)
    ka_ref[:, 0:hd] = k_ref[...].astype(BF16)
    ka_ref[:, hd:2 * hd] = (_dot(pieces, sel_k) + ones_k).astype(BF16)
    vb_ref[...] = v_ref[...].astype(BF16)

    ti = lax.broadcasted_iota(jnp.int32, (tq, tq), 0)
    si = lax.broadcasted_iota(jnp.int32, (tq, tq), 1)
    for n in range(seq // tq):
        r0, r1 = n * tq, (n + 1) * tq
        qa = qa_ref[r0:r1, :]
        u_d = jnp.where(si <= ti, _dot_nt(qa, ka_ref[r0:r1, :]), -jnp.inf)
        m = jnp.max(u_d, axis=1, keepdims=True)
        if n > 0:
            u_o = _dot_nt(qa, ka_ref[0:r0, :])
            m = jnp.maximum(m, jnp.max(u_o, axis=1, keepdims=True))
        p_d = jnp.exp2((u_d - m) * c1)
        l = jnp.sum(p_d, axis=1, keepdims=True)
        acc = _dot(p_d.astype(BF16), vb_ref[r0:r1, :])
        if n > 0:
            p_o = jnp.exp2((u_o - m) * c1)
            l = l + jnp.sum(p_o, axis=1, keepdims=True)
            acc = acc + _dot(p_o.astype(BF16), vb_ref[0:r0, :])
        o_ref[r0:r1, :] = (acc / l).astype(o_ref.dtype)


def fox_flash(q, k_stack, v_stack, slot, fcum, *, batch, seq, n_heads, tq):
    hd = FOX_HEAD_DIM
    kern = functools.partial(_fox_flash_kernel, seq=seq, tq=tq)
    return pl.pallas_call(
        kern,
        grid=(batch, n_heads),
        in_specs=[
            pl.BlockSpec((seq, hd), lambda b, h: (b, h)),
            pl.BlockSpec((None, seq, hd), lambda b, h: (slot, b, h)),
            pl.BlockSpec((None, seq, hd), lambda b, h: (slot, b, h)),
            pl.BlockSpec((seq, LANES), lambda b, h: (b, 0)),
        ],
        out_specs=pl.BlockSpec((seq, hd), lambda b, h: (b, h)),
        out_shape=jax.ShapeDtypeStruct((batch * seq, n_heads * hd), BF16),
        scratch_shapes=[pltpu.VMEM((seq, 2 * hd), BF16), pltpu.VMEM((seq, 2 * hd), BF16),
                        pltpu.VMEM((seq, hd), BF16)],
        compiler_params=_cparams(("parallel", "parallel")),
        name="fox_flash",
    )(q, k_stack, v_stack, fcum)


def _past_bias_kernel(pt_ref, u_ref, *rest, pages_per_step):
    pp = pages_per_step
    lf_refs = rest[0:pp]
    o_ref = rest[pp]
    run_ref = rest[pp + 1]

    @pl.when(pl.program_id(1) == 0)
    def _():
        run_ref[...] = jnp.zeros_like(run_ref)

    for p in range(pp):
        lf = lf_refs[p][...]
        su = _dot01(lf, u_ref[...])
        run = run_ref[...]
        o_ref[pp - 1 - p] = run + su[:, :LANES]
        run_ref[...] = run + su[:, LANES:]


def fox_past_bias(cache_logf_t, page_table, layer, *, pages_per_step):
    batch, n_pages = page_table.shape
    n_heads = cache_logf_t.shape[2]
    pp = pages_per_step
    nblk = n_pages // pp
    pos = jnp.arange(PAGE_SIZE)
    u = jnp.concatenate([(pos[:, None] > pos[None, :]), jnp.ones((PAGE_SIZE, PAGE_SIZE), bool)],
                        axis=1).astype(BF16)

    def page_map(p):
        return lambda b, j, pt: (layer, pt[b, n_pages - 1 - (j * pp + p)], 0, 0)

    grid_spec = pltpu.PrefetchScalarGridSpec(
        num_scalar_prefetch=1,
        grid=(batch, nblk),
        in_specs=[pl.BlockSpec((PAGE_SIZE, 2 * PAGE_SIZE), lambda b, j, pt: (0, 0))]
        + [pl.BlockSpec((None, None, n_heads, PAGE_SIZE), page_map(p)) for p in range(pp)],
        out_specs=pl.BlockSpec((None, pp, n_heads, PAGE_SIZE), lambda b, j, pt: (b, nblk - 1 - j, 0, 0)),
        scratch_shapes=[pltpu.VMEM((n_heads, PAGE_SIZE), F32)],
    )
    return pl.pallas_call(
        functools.partial(_past_bias_kernel, pages_per_step=pp),
        grid_spec=grid_spec,
        out_shape=jax.ShapeDtypeStruct((batch, n_pages, n_heads, PAGE_SIZE), F32),
        compiler_params=_cparams(("parallel", "arbitrary")),
        name="fox_past_bias",
    )(page_table, u, *([cache_logf_t] * pp))


def _decode_kernel(pt_ref, q_ref, kn_ref, vn_ref, fcol_ref, frow_ref, trow_ref, *rest, n_heads, pages_per_step):
    pp = pages_per_step
    k_refs = rest[0:pp]
    v_refs = rest[pp:2 * pp]
    o_ref = rest[2 * pp]
    m_ref, l_ref, acc_ref, mb_ref, qb_ref = rest[2 * pp + 1:]
    j = pl.program_id(1)
    c1 = LOG2E * FOX_HEAD_DIM ** -0.5
    bias_scale = FOX_HEAD_DIM ** 0.5
    rows = q_ref.shape[0]
    cols = k_refs[0].shape[0]

    @pl.when(j == 0)
    def _():
        r = lax.broadcasted_iota(jnp.int32, (rows, cols), 0)
        c = lax.broadcasted_iota(jnp.int32, (rows, cols), 1)
        same_head = (r % n_heads) == (c % n_heads)
        mb_ref[...] = jnp.where(same_head, fcol_ref[...] * bias_scale, -jnp.inf)
        qb = q_ref[...].astype(BF16)
        qb_ref[...] = qb
        s = _dot_nt(qb, kn_ref[...].astype(BF16)) + mb_ref[:, 0:rows] - frow_ref[...] * bias_scale
        rn = lax.broadcasted_iota(jnp.int32, (rows, rows), 0)
        cn = lax.broadcasted_iota(jnp.int32, (rows, rows), 1)
        s = jnp.where((cn // n_heads) <= (rn // n_heads), s, -jnp.inf)
        m_new = jnp.max(s, axis=1, keepdims=True)
        p = jnp.exp2((s - m_new) * c1)
        m_ref[...] = m_new
        l_ref[...] = jnp.sum(p, axis=1, keepdims=True)
        acc_ref[...] = _dot(p.astype(BF16), vn_ref[...].astype(BF16))

    s_pages = [_dot_nt(qb_ref[...], k_refs[pg][...].astype(BF16)) + mb_ref[...] + trow_ref[pg] * bias_scale
               for pg in range(pp)]
    m_old = m_ref[...]
    m_new = m_old
    for s in s_pages:
        m_new = jnp.maximum(m_new, jnp.max(s, axis=1, keepdims=True))
    alpha = jnp.exp2((m_old - m_new) * c1)
    l_new = alpha * l_ref[...]
    acc_new = alpha * acc_ref[...]
    for pg, s in enumerate(s_pages):
        p = jnp.exp2((s - m_new) * c1)
        l_new = l_new + jnp.sum(p, axis=1, keepdims=True)
        acc_new = acc_new + _dot(p.astype(BF16), v_refs[pg][...].astype(BF16))
    l_ref[...] = l_new
    acc_ref[...] = acc_new
    m_ref[...] = m_new

    @pl.when(j == pl.num_programs(1) - 1)
    def _():
        o_ref[...] = (acc_ref[...] / l_ref[...]).astype(o_ref.dtype)


def fox_decode(q, k_new, v_new, fnew, past_bias, cache_k, cache_v, page_table, layer, *, batch, n_new, n_heads,
               pages_per_step):
    hd = FOX_HEAD_DIM
    d = n_heads * hd
    n_pages = page_table.shape[1]
    pp = pages_per_step
    rows = n_new * n_heads
    cols = PAGE_SIZE * n_heads
    q2 = q.reshape(batch, rows, hd)
    kn2 = k_new.reshape(batch, rows, hd)
    vn2 = v_new.reshape(batch, rows, hd)
    fcol = fnew.reshape(batch, rows, 1)
    frow = fnew.reshape(batch, 1, rows)
    trow = jnp.swapaxes(past_bias, 2, 3).reshape(batch, n_pages, 1, cols)

    def page_map(p):
        return lambda b, j, pt: (layer, pt[b, j * pp + p], 0, 0)

    in_specs = [
        pl.BlockSpec((None, rows, hd), lambda b, j, pt: (b, 0, 0)),
        pl.BlockSpec((None, rows, hd), lambda b, j, pt: (b, 0, 0)),
        pl.BlockSpec((None, rows, hd), lambda b, j, pt: (b, 0, 0)),
        pl.BlockSpec((None, rows, 1), lambda b, j, pt: (b, 0, 0)),
        pl.BlockSpec((None, 1, rows), lambda b, j, pt: (b, 0, 0)),
        pl.BlockSpec((None, pp, 1, cols), lambda b, j, pt: (b, j, 0, 0)),
    ]
    in_specs += [pl.BlockSpec((None, None, cols, hd), page_map(p)) for p in range(pp)]
    in_specs += [pl.BlockSpec((None, None, cols, hd), page_map(p)) for p in range(pp)]
    grid_spec = pltpu.PrefetchScalarGridSpec(
        num_scalar_prefetch=1,
        grid=(batch, n_pages // pp),
        in_specs=in_specs,
        out_specs=pl.BlockSpec((None, rows, hd), lambda b, j, pt: (b, 0, 0)),
        scratch_shapes=[
            pltpu.VMEM((rows, 1), F32),
            pltpu.VMEM((rows, 1), F32),
            pltpu.VMEM((rows, hd), F32),
            pltpu.VMEM((rows, cols), F32),
            pltpu.VMEM((rows, hd), BF16),
        ],
    )
    out = pl.pallas_call(
        functools.partial(_decode_kernel, n_heads=n_heads, pages_per_step=pp),
        grid_spec=grid_spec,
        out_shape=jax.ShapeDtypeStruct((batch, rows, hd), BF16),
        compiler_params=_cparams(("parallel", "arbitrary")),
        name="fox_decode",
    )(page_table, q2, kn2, vn2, fcol, frow, trow, *([cache_k] * pp), *([cache_v] * pp))
    return out.reshape(batch * n_new, d)


def kernel(x_prompt, x_sample, cache_k, cache_v, cache_logf, state_conv, state_ssm, page_table, ln_mix_pre, ln_mix_post, ln_ffn_pre, ln_ffn_post, ssd_w_in, ssd_conv_w, ssd_conv_b, ssd_dt_bias, ssd_a_log, ssd_d, ssd_norm, ssd_w_out, fox_w_in, fox_b_f, fox_w_out, ffn_w_gate, ffn_w_up, ffn_w_down):
    bp, lp, d = x_prompt.shape
    bs, ls, _ = x_sample.shape
    depth = ln_mix_pre.shape[0]
    n_att, n_pool, page, fox_heads, fox_hd = cache_k.shape
    ssd_heads = ssd_dt_bias.shape[1]
    d_inner = ssd_heads * SSD_HEAD_DIM
    conv_dim = ssd_conv_w.shape[2]
    mp, ms = bp * lp, bs * ls
    tmp, tms = min(mp, 1024), min(ms, 1024)

    hp = x_prompt.reshape(mp, d)
    hs = x_sample.reshape(ms, d)
    ck = cache_k.reshape(n_att, n_pool, page * fox_heads, fox_hd)
    cv = cache_v.reshape(n_att, n_pool, page * fox_heads, fox_hd)
    clf_t = jnp.swapaxes(cache_logf, 2, 3)

    zero_conv = jnp.zeros((bp, SUBLANES, conv_dim), F32)
    zero_ssm = jnp.zeros((1, bp, ssd_heads, SSD_HEAD_DIM, D_STATE), F32)
    conv_init_s = jnp.pad(state_conv, ((0, 0), (0, 0), (SUBLANES - (CONV_W - 1), 0), (0, 0)))
    ls_pad = SSD_CHUNK

    w_ssd_out, w_fox_out = ssd_w_out.astype(BF16), fox_w_out.astype(BF16)

    fp_l, cp_l, sp_l = [], [], []
    fs_l, cs_l, ss_l = [], [], []
    kst_p = vst_p = kst_s = vst_s = None
    for i in range(depth):
        j = i // 2
        if i % 2 == 0:
            par = (ssd_conv_w[j], ssd_conv_b[j], ssd_dt_bias[j], ssd_a_log[j], ssd_d[j], ssd_norm[j])
            zx_s, w_in = norm_matmul(hs, ln_mix_pre[i], ssd_w_in, j, tm=tms, tn=SSD_IN_TN)
            zx_p = norm_matmul(hp, ln_mix_pre[i], w_in, 0, tm=tmp, tn=SSD_IN_TN)
            y_p, st_p = ssd_core(zx_p, zero_conv, zero_ssm, 0, *par, batch=bp, seq=lp, valid_rows=SSD_CHUNK,
                                 groups_per_step=SSD_GROUPS_PER_STEP)
            zx_s3 = zx_s.reshape(bs, ls, -1)
            zx_sp = jnp.pad(zx_s3, ((0, 0), (0, ls_pad - ls), (0, 0))).reshape(bs * ls_pad, -1)
            y_sp, st_s = ssd_core(zx_sp, conv_init_s[j], state_ssm, j, *par, batch=bs, seq=ls_pad, valid_rows=ls,
                                  groups_per_step=SSD_GROUPS_PER_STEP)
            y_s = y_sp.reshape(bs, ls_pad, d_inner)[:, :ls].reshape(ms, d_inner)
            hp = matmul_norm_res(y_p, w_ssd_out, j, ln_mix_post[i], hp, tm=min(mp, 256))
            hs = matmul_norm_res(y_s, w_ssd_out, j, ln_mix_post[i], hs, tm=tms)
            cp_l.append(zx_p.reshape(bp, lp, -1)[:, lp - (CONV_W - 1):, d_inner:d_inner + conv_dim])
            cs_l.append(zx_s3[:, ls - (CONV_W - 1):, d_inner:d_inner + conv_dim])
            sp_l.append(st_p)
            ss_l.append(st_s)
        else:
            q_s, kst_s, vst_s, fl_s, w_in, w_inf = fox_in_proj(
                hs, ln_mix_pre[i], fox_w_in, fox_w_in, j, kst_s, vst_s, j, n_att, tm=tms, tn=FOX_IN_TN, n_heads=fox_heads)
            q_p, kst_p, vst_p, fl_p = fox_in_proj(
                hp, ln_mix_pre[i], w_in, w_inf, 0, kst_p, vst_p, j, n_att, tm=min(mp, 512), tn=FOX_IN_TN,
                n_heads=fox_heads)
            lf_p, fc_p = fox_forget(fl_p, fox_b_f[j], batch=bp, seq=lp)
            lf_s, fc_s = fox_forget(fl_s, fox_b_f[j], batch=bs, seq=ls)
            a_p = fox_flash(q_p, kst_p, vst_p, j, fc_p, batch=bp, seq=lp, n_heads=fox_heads, tq=256)
            tb_s = fox_past_bias(clf_t, page_table, j, pages_per_step=16)
            a_s = fox_decode(q_s, kst_s[j], vst_s[j], fc_s[:, :fox_heads], tb_s, ck, cv, page_table, j, batch=bs,
                             n_new=ls, n_heads=fox_heads, pages_per_step=8)
            hp = matmul_norm_res(a_p, w_fox_out, j, ln_mix_post[i], hp, tm=min(mp, 512))
            hs = matmul_norm_res(a_s, w_fox_out, j, ln_mix_post[i], hs, tm=tms)
            fp_l.append(lf_p.reshape(bp, lp, fox_heads))
            fs_l.append(lf_s.reshape(bs, ls, fox_heads))
        hs, w_gate, w_up, w_down = ffn(hs, ln_ffn_pre[i], ffn_w_gate, ffn_w_up, ffn_w_down, i, ln_ffn_post[i],
                                       tm=tms, tf=512)
        hp = ffn(hp, ln_ffn_pre[i], w_gate, w_up, w_down, 0, ln_ffn_post[i], tm=min(mp, 512), tf=512)
    kv_p = lambda a: a.reshape(n_att, bp, lp, fox_heads, fox_hd)
    kv_s = lambda a: a.reshape(n_att, bs, ls, fox_heads, fox_hd)
    return (hp.reshape(bp, lp, d), hs.reshape(bs, ls, d),
            kv_p(kst_p), kv_p(vst_p), jnp.stack(fp_l), jnp.stack(cp_l), jnp.stack(sp_l),
            kv_s(kst_s), kv_s(vst_s), jnp.stack(fs_l), jnp.stack(cs_l), jnp.stack(ss_l))
```

```python
import functools
import types

import jax
import jax.numpy as jnp
from jax import lax
from jax.experimental import pallas as pl
from jax.experimental.pallas import tpu as pltpu

F32 = jnp.float32
BF16 = jnp.bfloat16

RMS_EPS = 1e-6
LOG2E = 1.4426950408889634
LANES = 128
SUBLANES = 8
SSD_HEAD_DIM = 64
SSD_GROUPS = 8
D_STATE = 128
CONV_W = 4
SSD_CHUNK = 128
FOX_HEAD_DIM = 128
PAGE_SIZE = 128
VMEM_LIMIT = 60 * 1024 * 1024
SSD_IN_TN = 1152
FOX_IN_TN = 1024
SSD_GROUPS_PER_STEP = 4
PHASES = ("init", "main", "final")


def _cparams(sem):
    return pltpu.CompilerParams(dimension_semantics=sem, vmem_limit_bytes=VMEM_LIMIT)


def _sigmoid(x):
    return 1.0 / (1.0 + jnp.exp(-x))


def _softplus(x):
    return jnp.maximum(x, 0.0) + jnp.log1p(jnp.exp(-jnp.abs(x)))


def _rms(x, g):
    ms = jnp.mean(x * x, axis=-1, keepdims=True)
    return x * lax.rsqrt(ms + RMS_EPS) * g


def _dot(a, b):
    return jnp.dot(a, b, preferred_element_type=F32)


def _dot_nt(a, b):
    return lax.dot_general(a, b, (((1,), (1,)), ((), ())), preferred_element_type=F32)


def _split3(x):
    hi = x.astype(BF16)
    r1 = x - hi.astype(F32)
    mid = r1.astype(BF16)
    lo = (r1 - mid.astype(F32)).astype(BF16)
    return hi, mid, lo


def _dot01(x, m01):
    hi, mid, lo = _split3(x)
    return (_dot(lo, m01) + _dot(mid, m01)) + _dot(hi, m01)


def _cumsum_rows(x, n_rows):
    row = lax.broadcasted_iota(jnp.int32, x.shape, 0)
    s = 1
    while s < n_rows:
        x = x + jnp.where(row >= s, pltpu.roll(x, s, 0), 0.0)
        s *= 2
    return x


def _norm_matmul_kernel(x_ref, g_ref, w_ref, o_ref, *rest):
    *maybe_wb_ref, xn_ref = rest

    @pl.when(pl.program_id(1) == 0)
    def _():
        xn_ref[...] = _rms(x_ref[...], g_ref[...]).astype(BF16)

    wb = w_ref[...].astype(BF16)
    for wb_ref in maybe_wb_ref:
        wb_ref[...] = wb
    o_ref[...] = _dot(xn_ref[...], wb)


def norm_matmul(x, g, w, layer, *, tm, tn):
    m, d = x.shape
    n = w.shape[2]
    emit_wb = w.dtype != BF16
    assert not emit_wb or m == tm
    out_specs = [pl.BlockSpec((tm, tn), lambda i, j: (i, j))]
    out_shape = [jax.ShapeDtypeStruct((m, n), F32)]
    if emit_wb:
        out_specs.append(pl.BlockSpec((None, d, tn), lambda i, j: (0, 0, j)))
        out_shape.append(jax.ShapeDtypeStruct((1, d, n), BF16))
    res = pl.pallas_call(
        _norm_matmul_kernel,
        grid=(m // tm, pl.cdiv(n, tn)),
        in_specs=[
            pl.BlockSpec((tm, d), lambda i, j: (i, 0)),
            pl.BlockSpec((1, d), lambda i, j: (0, 0)),
            pl.BlockSpec((None, d, tn), lambda i, j: (layer, 0, j)),
        ],
        out_specs=out_specs,
        out_shape=out_shape,
        scratch_shapes=[pltpu.VMEM((tm, d), BF16)],
        compiler_params=_cparams(("parallel", "arbitrary")),
        name="norm_matmul",
    )(x, g.reshape(1, d), w)
    return res if emit_wb else res[0]


def _fox_in_proj_kernel(x_ref, g_ref, w_ref, wf_ref, *rest, tiles_per_part, emit_wb, aliased):
    rest = rest[2:] if aliased else rest
    q_ref, k_ref, v_ref, f_ref = rest[:4]
    xn_ref = rest[-1]
    j = pl.program_id(1)
    wb = w_ref[...].astype(BF16)
    if emit_wb:
        wb_ref, wfb_ref = rest[4:6]
        wb_ref[...] = wb

    @pl.when(j == 0)
    def _():
        xn = _rms(x_ref[...], g_ref[...]).astype(BF16)
        xn_ref[...] = xn
        wfb = wf_ref[...].astype(BF16)
        if emit_wb:
            wfb_ref[...] = wfb
        f_ref[...] = _dot(xn, wfb)

    y = _dot(xn_ref[...], wb)

    @pl.when(j < tiles_per_part)
    def _():
        q_ref[...] = y

    @pl.when((j >= tiles_per_part) & (j < 2 * tiles_per_part))
    def _():
        k_ref[...] = y

    @pl.when(j >= 2 * tiles_per_part)
    def _():
        v_ref[...] = y


def fox_in_proj(x, g, w, w_f, layer, k_stack, v_stack, slot, n_slots, *, tm, tn, n_heads):
    m, d = x.shape
    n = w.shape[2]
    tpp = d // tn
    f_blk = (n - n_heads) // LANES
    emit_wb = w.dtype != BF16
    aliased = k_stack is not None
    assert not emit_wb or m == tm
    part = lambda p: (lambda i, j: (i, jnp.clip(j - p * tpp, 0, tpp - 1)))
    spart = lambda p: (lambda i, j: (slot, i, jnp.clip(j - p * tpp, 0, tpp - 1)))
    in_specs = [
        pl.BlockSpec((tm, d), lambda i, j: (i, 0)),
        pl.BlockSpec((1, d), lambda i, j: (0, 0)),
        pl.BlockSpec((None, d, tn), lambda i, j: (layer, 0, j)),
        pl.BlockSpec((None, d, LANES), lambda i, j: (layer, 0, f_blk)),
    ]
    args = [x, g.reshape(1, d), w, w_f]
    aliases = {}
    if aliased:
        in_specs += [pl.BlockSpec(memory_space=pl.ANY), pl.BlockSpec(memory_space=pl.ANY)]
        args += [k_stack, v_stack]
        aliases = {4: 1, 5: 2}
    out_specs = [
        pl.BlockSpec((tm, tn), part(0)),
        pl.BlockSpec((None, tm, tn), spart(1)),
        pl.BlockSpec((None, tm, tn), spart(2)),
        pl.BlockSpec((tm, LANES), lambda i, j: (i, 0)),
    ]
    out_shape = [
        jax.ShapeDtypeStruct((m, d), F32),
        jax.ShapeDtypeStruct((n_slots, m, d), F32),
        jax.ShapeDtypeStruct((n_slots, m, d), F32),
        jax.ShapeDtypeStruct((m, LANES), F32),
    ]
    if emit_wb:
        out_specs += [pl.BlockSpec((None, d, tn), lambda i, j: (0, 0, j)),
                      pl.BlockSpec((None, d, LANES), lambda i, j: (0, 0, f_blk))]
        out_shape += [jax.ShapeDtypeStruct((1, d, n), BF16), jax.ShapeDtypeStruct((1, d, n), BF16)]
    res = pl.pallas_call(
        functools.partial(_fox_in_proj_kernel, tiles_per_part=tpp, emit_wb=emit_wb, aliased=aliased),
        grid=(m // tm, 3 * tpp),
        in_specs=in_specs,
        out_specs=out_specs,
        out_shape=out_shape,
        input_output_aliases=aliases,
        scratch_shapes=[pltpu.VMEM((tm, d), BF16)],
        compiler_params=_cparams(("arbitrary", "arbitrary")),
        name="fox_in_proj",
    )(*args)
    return res


def _matmul_norm_res_kernel(y_ref, w_ref, g_ref, h_ref, o_ref):
    o_ref[...] = h_ref[...] + _rms(_dot(y_ref[...], w_ref[...]), g_ref[...])


def matmul_norm_res(y, w, layer, g, h, *, tm):
    m, kd = y.shape
    d = w.shape[2]
    return pl.pallas_call(
        _matmul_norm_res_kernel,
        grid=(m // tm,),
        in_specs=[
            pl.BlockSpec((tm, kd), lambda i: (i, 0)),
            pl.BlockSpec((None, kd, d), lambda i: (layer, 0, 0)),
            pl.BlockSpec((1, d), lambda i: (0, 0)),
            pl.BlockSpec((tm, d), lambda i: (i, 0)),
        ],
        out_specs=pl.BlockSpec((tm, d), lambda i: (i, 0)),
        out_shape=jax.ShapeDtypeStruct((m, d), F32),
        compiler_params=_cparams(("parallel",)),
        name="matmul_norm_res",
    )(y, w, g.reshape(1, d), h)


def _ffn_kernel(h_ref, gpre_ref, wg_ref, wu_ref, wd_ref, gpost_ref, o_ref, *rest):
    *maybe_wb_refs, xn_ref, acc_ref = rest
    f = pl.program_id(1)

    @pl.when(f == 0)
    def _():
        xn_ref[...] = _rms(h_ref[...], gpre_ref[...]).astype(BF16)
        acc_ref[...] = jnp.zeros_like(acc_ref)

    wg, wu, wd = (r[...].astype(BF16) for r in (wg_ref, wu_ref, wd_ref))
    for wb_ref, wb in zip(maybe_wb_refs, (wg, wu, wd)):
        wb_ref[...] = wb
    xn = xn_ref[...]
    gate = _dot(xn, wg)
    up = _dot(xn, wu)
    act = (gate * _sigmoid(gate) * up).astype(BF16)
    acc_ref[...] += _dot(act, wd)

    @pl.when(f == pl.num_programs(1) - 1)
    def _():
        o_ref[...] = h_ref[...] + _rms(acc_ref[...], gpost_ref[...])


def ffn(h, g_pre, w_gate, w_up, w_down, layer, g_post, *, tm, tf):
    m, d = h.shape
    dff = w_gate.shape[2]
    emit_wb = w_gate.dtype != BF16
    assert not emit_wb or m == tm
    out_specs = [pl.BlockSpec((tm, d), lambda i, f: (i, 0))]
    out_shape = [jax.ShapeDtypeStruct((m, d), F32)]
    if emit_wb:
        out_specs += [pl.BlockSpec((None, d, tf), lambda i, f: (0, 0, f)),
                      pl.BlockSpec((None, d, tf), lambda i, f: (0, 0, f)),
                      pl.BlockSpec((None, tf, d), lambda i, f: (0, f, 0))]
        out_shape += [jax.ShapeDtypeStruct((1, d, dff), BF16), jax.ShapeDtypeStruct((1, d, dff), BF16),
                      jax.ShapeDtypeStruct((1, dff, d), BF16)]
    res = pl.pallas_call(
        _ffn_kernel,
        grid=(m // tm, dff // tf),
        in_specs=[
            pl.BlockSpec((tm, d), lambda i, f: (i, 0)),
            pl.BlockSpec((1, d), lambda i, f: (0, 0)),
            pl.BlockSpec((None, d, tf), lambda i, f: (layer, 0, f)),
            pl.BlockSpec((None, d, tf), lambda i, f: (layer, 0, f)),
            pl.BlockSpec((None, tf, d), lambda i, f: (layer, f, 0)),
            pl.BlockSpec((1, d), lambda i, f: (0, 0)),
        ],
        out_specs=out_specs,
        out_shape=out_shape,
        scratch_shapes=[pltpu.VMEM((tm, d), BF16), pltpu.VMEM((tm, d), F32)],
        compiler_params=_cparams(("parallel", "arbitrary")),
        name="ffn",
    )(h, g_pre.reshape(1, d), w_gate, w_up, w_down, g_post.reshape(1, d))
    return res if emit_wb else res[0]


def _conv_silu(pad_ref, x, w_ref, b_ref, L):
    pad_ref[SUBLANES:SUBLANES + L, :] = x
    acc = b_ref[...] + x * w_ref[CONV_W - 1:CONV_W, :]
    for k in range(1, CONV_W):
        acc = acc + pad_ref[SUBLANES - k:SUBLANES - k + L, :] * w_ref[CONV_W - 1 - k:CONV_W - k, :]
    pad_ref[0:SUBLANES, :] = pad_ref[L:L + SUBLANES, :]
    return acc * _sigmoid(acc)


def _ssd_kernel(z_ref, x_ref, b_ref, c_ref, dt_ref,
                cwx_ref, cwb_ref, cwc_ref, cbx_ref, cbb_ref, cbc_ref,
                cix_ref, cib_ref, cic_ref,
                dtb_ref, alog_ref, dsk_ref, ng_ref, exp_ref, h0_ref,
                y_ref, hout_ref,
                xpad_ref, bpad_ref, cpad_ref, st_ref, *, L, valid_rows, n_heads, gw, phases=PHASES):
    c = pl.program_id(2)
    width = x_ref.shape[1]
    n_grp = width // gw
    heads = gw // SSD_HEAD_DIM

    if "init" in phases:
        @pl.when(c == 0)
        def _():
            xpad_ref[0:SUBLANES, :] = cix_ref[...]
            bpad_ref[0:SUBLANES, :] = cib_ref[...]
            cpad_ref[0:SUBLANES, :] = cic_ref[...]
            st_ref[...] = h0_ref[...].reshape(width, D_STATE).T

    if "main" in phases:
        _ssd_chunk(z_ref, x_ref, b_ref, c_ref, dt_ref, cwx_ref, cwb_ref, cwc_ref, cbx_ref, cbb_ref, cbc_ref,
                   dtb_ref, alog_ref, dsk_ref, ng_ref, exp_ref, y_ref, xpad_ref, bpad_ref, cpad_ref, st_ref,
                   L=L, valid_rows=valid_rows, n_heads=n_heads, gw=gw)

    if "final" in phases:
        @pl.when(c == pl.num_programs(2) - 1)
        def _():
            hout_ref[...] = st_ref[...].T.reshape(n_grp * heads, SSD_HEAD_DIM, D_STATE)


def _ssd_chunk(z_ref, x_ref, b_ref, c_ref, dt_ref, cwx_ref, cwb_ref, cwc_ref, cbx_ref, cbb_ref, cbc_ref,
               dtb_ref, alog_ref, dsk_ref, ng_ref, exp_ref, y_ref, xpad_ref, bpad_ref, cpad_ref, st_ref,
               *, L, valid_rows, n_heads, gw):
    width = x_ref.shape[1]
    n_grp = width // gw
    heads = gw // SSD_HEAD_DIM
    xs = _conv_silu(xpad_ref, x_ref[...], cwx_ref, cbx_ref, L)
    bm = _conv_silu(bpad_ref, b_ref[...], cwb_ref, cbb_ref, L)
    cm = _conv_silu(cpad_ref, c_ref[...], cwc_ref, cbc_ref, L)

    lane = lax.broadcasted_iota(jnp.int32, (L, LANES), 1)
    row = lax.broadcasted_iota(jnp.int32, (L, LANES), 0)
    dt = _softplus(dt_ref[...] + dtb_ref[...])
    dt = jnp.where((lane < n_heads) & (row < valid_rows), dt, 0.0)
    dta = dt * (-jnp.exp(alog_ref[...]))
    acum = _cumsum_rows(dta, L)
    expand = exp_ref[...]
    dt_w = _dot01(dt, expand)
    acum_w = _dot01(acum, expand)
    a_last = acum_w[L - 1:L, :]

    xdt = xs * dt_w
    xdec = (xdt * jnp.exp(a_last - acum_w)).astype(BF16)
    exp_acum = jnp.exp(acum_w)
    exp_last = jnp.exp(a_last)
    acum_t = acum_w.T
    li = lax.broadcasted_iota(jnp.int32, (L, L), 0)
    si = lax.broadcasted_iota(jnp.int32, (L, L), 1)
    tril = si <= li
    low_half = lax.broadcasted_iota(jnp.int32, (L, LANES), 1) < SSD_HEAD_DIM
    z = z_ref[...]
    gate = z * _sigmoid(z)
    for gi in range(n_grp):
        g0 = gi * gw
        n0 = gi * D_STATE
        bm_g = bm[:, n0:n0 + D_STATE]
        cm_bf = cm[:, n0:n0 + D_STATE].astype(BF16)
        cb = _dot_nt(cm_bf, bm_g.astype(BF16))
        y_pairs = []
        for pr in range(heads // 2):
            p0 = g0 + pr * LANES
            xp = xdt[:, p0:p0 + LANES].astype(BF16)
            y_two = []
            for r0 in (p0, p0 + SSD_HEAD_DIM):
                col = acum_w[:, r0:r0 + 1]
                rowv = acum_t[r0:r0 + 1, :]
                decay = jnp.exp(jnp.where(tril, col - rowv, -jnp.inf))
                y_two.append(_dot((cb * decay).astype(BF16), xp))
            y_pairs.append(jnp.where(low_half, y_two[0], y_two[1]))
        y_diag = jnp.concatenate(y_pairs, axis=1)

        st = st_ref[:, g0:g0 + gw]
        y_off = exp_acum[:, g0:g0 + gw] * _dot(cm_bf, st.astype(BF16))
        st_new = exp_last[:, g0:g0 + gw] * st + _dot(bm_g.T.astype(BF16), xdec[:, g0:g0 + gw])
        st_ref[:, g0:g0 + gw] = st_new

        y = y_diag + y_off + xs[:, g0:g0 + gw] * dsk_ref[:, g0:g0 + gw]
        y = y * gate[:, g0:g0 + gw]
        y_ref[:, g0:g0 + gw] = _rms(y, ng_ref[:, g0:g0 + gw]).astype(y_ref.dtype)


def _ssd_decode_kernel(pt_ref, *refs, ssd_kernel, decode_kernel, n_ssd_in, n_dec_in, n_ssd_scratch):
    ssd_in, dec_in = refs[:n_ssd_in], refs[n_ssd_in:n_ssd_in + n_dec_in]
    rest = refs[n_ssd_in + n_dec_in:]
    ssd_out, dec_out, scr = rest[:2], rest[2:3], rest[3:]
    ssd_refs = (*ssd_in, *ssd_out, *scr[:n_ssd_scratch])
    dec_refs = (pt_ref, *dec_in, *dec_out, *scr[n_ssd_scratch:])
    for phase in PHASES:
        ssd_kernel(*ssd_refs, phases=(phase,))
        decode_kernel(*dec_refs, phases=(phase,))


def ssd_core(zx, conv_init, h0, layer, conv_w, conv_b, dt_bias, a_log, d_skip, norm_g,
             *, batch, seq, valid_rows, groups_per_step, decode=None):
    n_heads = dt_bias.shape[0]
    d_inner = n_heads * SSD_HEAD_DIM
    gw = d_inner // SSD_GROUPS
    gps = groups_per_step
    wx = gps * gw
    wn = gps * D_STATE
    hs = gps * (n_heads // SSD_GROUPS)
    L = min(SSD_CHUNK, seq)
    nc = seq // L
    xb0 = d_inner // wx
    bb0 = (2 * d_inner) // wn
    cb0 = (2 * d_inner + SSD_GROUPS * D_STATE) // wn
    dtb0 = (2 * d_inner + 2 * SSD_GROUPS * D_STATE) // LANES
    cwb0 = d_inner // wn
    cwc0 = (d_inner + SSD_GROUPS * D_STATE) // wn

    pad_h = LANES - n_heads
    dtb = jnp.pad(dt_bias.astype(F32), (0, pad_h)).reshape(1, LANES)
    alog = jnp.pad(a_log.astype(F32), (0, pad_h)).reshape(1, LANES)
    dsk = jnp.repeat(d_skip.astype(F32), SSD_HEAD_DIM).reshape(1, d_inner)
    head_of_lane = jnp.arange(d_inner) // SSD_HEAD_DIM
    expand = (jnp.arange(LANES)[:, None] == head_of_lane[None, :]).astype(BF16)
    conv_b2 = conv_b.reshape(1, -1)

    row_blk = lambda b, g, c, *_: b * nc + c
    kern = functools.partial(_ssd_kernel, L=L, valid_rows=valid_rows, n_heads=n_heads, gw=gw)
    n_gsteps = SSD_GROUPS // gps
    grid = (batch, n_gsteps, nc)
    in_specs = [
            pl.BlockSpec((L, wx), lambda b, g, c, *_: (row_blk(b, g, c), g)),
            pl.BlockSpec((L, wx), lambda b, g, c, *_: (row_blk(b, g, c), xb0 + g)),
            pl.BlockSpec((L, wn), lambda b, g, c, *_: (row_blk(b, g, c), bb0 + g)),
            pl.BlockSpec((L, wn), lambda b, g, c, *_: (row_blk(b, g, c), cb0 + g)),
            pl.BlockSpec((L, LANES), lambda b, g, c, *_: (row_blk(b, g, c), dtb0)),
            pl.BlockSpec((CONV_W, wx), lambda b, g, c, *_: (0, g)),
            pl.BlockSpec((CONV_W, wn), lambda b, g, c, *_: (0, cwb0 + g)),
            pl.BlockSpec((CONV_W, wn), lambda b, g, c, *_: (0, cwc0 + g)),
            pl.BlockSpec((1, wx), lambda b, g, c, *_: (0, g)),
            pl.BlockSpec((1, wn), lambda b, g, c, *_: (0, cwb0 + g)),
            pl.BlockSpec((1, wn), lambda b, g, c, *_: (0, cwc0 + g)),
            pl.BlockSpec((None, SUBLANES, wx), lambda b, g, c, *_: (b, 0, g)),
            pl.BlockSpec((None, SUBLANES, wn), lambda b, g, c, *_: (b, 0, cwb0 + g)),
            pl.BlockSpec((None, SUBLANES, wn), lambda b, g, c, *_: (b, 0, cwc0 + g)),
            pl.BlockSpec((1, LANES), lambda b, g, c, *_: (0, 0)),
            pl.BlockSpec((1, LANES), lambda b, g, c, *_: (0, 0)),
            pl.BlockSpec((1, wx), lambda b, g, c, *_: (0, g)),
            pl.BlockSpec((1, wx), lambda b, g, c, *_: (0, g)),
            pl.BlockSpec((LANES, wx), lambda b, g, c, *_: (0, g)),
            pl.BlockSpec((None, None, hs, SSD_HEAD_DIM, D_STATE), lambda b, g, c, *_: (layer, b, g, 0, 0)),
    ]
    out_specs = [
        pl.BlockSpec((L, wx), lambda b, g, c, *_: (row_blk(b, g, c), g)),
        pl.BlockSpec((None, hs, SSD_HEAD_DIM, D_STATE), lambda b, g, c, *_: (b, g, 0, 0)),
    ]
    out_shape = [
        jax.ShapeDtypeStruct((batch * seq, d_inner), BF16),
        jax.ShapeDtypeStruct((batch, n_heads, SSD_HEAD_DIM, D_STATE), F32),
    ]
    scratch = [
        pltpu.VMEM((L + SUBLANES, wx), F32),
        pltpu.VMEM((L + SUBLANES, wn), F32),
        pltpu.VMEM((L + SUBLANES, wn), F32),
        pltpu.VMEM((D_STATE, wx), F32),
    ]
    args = (zx, zx, zx, zx, zx, conv_w, conv_w, conv_w, conv_b2, conv_b2, conv_b2,
            conv_init, conv_init, conv_init, dtb, alog, dsk, norm_g.reshape(1, d_inner), expand, h0)
    if decode is None:
        y, hout = pl.pallas_call(
            kern, grid=grid, in_specs=in_specs, out_specs=out_specs, out_shape=out_shape, scratch_shapes=scratch,
            compiler_params=_cparams(("parallel", "parallel", "arbitrary")), name="ssd_core",
        )(*args)
        return y, hout
    dec = decode(lambda b, g, c, pt: (b * n_gsteps + g, c), 2)
    assert (dec.batch, dec.n_steps) == (batch * n_gsteps, nc)
    fused = functools.partial(_ssd_decode_kernel, ssd_kernel=kern, decode_kernel=dec.kernel,
                              n_ssd_in=len(in_specs), n_dec_in=len(dec.in_specs), n_ssd_scratch=len(scratch))
    grid_spec = pltpu.PrefetchScalarGridSpec(
        num_scalar_prefetch=1, grid=grid, in_specs=in_specs + dec.in_specs,
        out_specs=out_specs + [dec.out_spec], scratch_shapes=scratch + dec.scratch)
    y, hout, att = pl.pallas_call(
        fused, grid_spec=grid_spec, out_shape=out_shape + [dec.out_shape],
        compiler_params=_cparams(("arbitrary", "arbitrary", "arbitrary")), name="ssd_core_fox_decode",
    )(dec.page_table, *args, *dec.args)
    return y, hout, dec.finish(att)


def _forget_kernel(p_ref, bf_ref, logf_ref, fcum_ref, *, seq, n_heads):
    lane = lax.broadcasted_iota(jnp.int32, (seq, LANES), 1)
    logit = p_ref[...] + bf_ref[...]
    logf = jnp.where(lane < n_heads, -_softplus(-logit), 0.0)
    logf_ref[...] = logf[:, :n_heads]
    fcum_ref[...] = _cumsum_rows(logf, seq)


def fox_forget(flog, b_f, *, batch, seq):
    n_heads = b_f.shape[0]
    bfp = jnp.pad(b_f.astype(F32), (0, LANES - n_heads)).reshape(1, LANES)
    return pl.pallas_call(
        functools.partial(_forget_kernel, seq=seq, n_heads=n_heads),
        grid=(batch,),
        in_specs=[pl.BlockSpec((seq, LANES), lambda b: (b, 0)),
                  pl.BlockSpec((1, LANES), lambda b: (0, 0))],
        out_specs=[pl.BlockSpec((seq, n_heads), lambda b: (b, 0)),
                   pl.BlockSpec((seq, LANES), lambda b: (b, 0))],
        out_shape=[jax.ShapeDtypeStruct((batch * seq, n_heads), F32),
                   jax.ShapeDtypeStruct((batch * seq, LANES), F32)],
        compiler_params=_cparams(("parallel",)),
        name="fox_forget",
    )(flog, bfp)


N_PIECES = 3


def _fox_flash_kernel(q_ref, k_ref, v_ref, f_ref, o_ref, qa_ref, ka_ref, vb_ref, *, seq, tq):
    h = pl.program_id(1)
    hd = FOX_HEAD_DIM
    c1 = LOG2E * hd ** -0.5
    pieces = jnp.concatenate(_split3(f_ref[...] * hd ** 0.5), axis=1)
    r = lax.broadcasted_iota(jnp.int32, (N_PIECES * LANES, LANES), 0)
    c = lax.broadcasted_iota(jnp.int32, (N_PIECES * LANES, LANES), 1)
    is_h = (r % LANES) == h
    sel_q = jnp.where(is_h & (c == r // LANES), 1.0, 0.0).astype(BF16)
    sel_k = jnp.where(is_h & (c == N_PIECES + r // LANES), -1.0, 0.0).astype(BF16)
    lane = lax.broadcasted_iota(jnp.int32, (seq, LANES), 1)
    ones_q = jnp.where((lane >= N_PIECES) & (lane < 2 * N_PIECES), 1.0, 0.0)
    ones_k = jnp.where(lane < N_PIECES, 1.0, 0.0)
    qa_ref[:, 0:hd] = q_ref[...].astype(BF16)
    qa_ref[:, hd:2 * hd] = (_dot(pieces, sel_q) + ones_q).astype(BF16)
    ka_ref[:, 0:hd] = k_ref[...].astype(BF16)
    ka_ref[:, hd:2 * hd] = (_dot(pieces, sel_k) + ones_k).astype(BF16)
    vb_ref[...] = v_ref[...].astype(BF16)

    ti = lax.broadcasted_iota(jnp.int32, (tq, tq), 0)
    si = lax.broadcasted_iota(jnp.int32, (tq, tq), 1)
    for n in range(seq // tq):
        r0, r1 = n * tq, (n + 1) * tq
        qa = qa_ref[r0:r1, :]
        u_d = jnp.where(si <= ti, _dot_nt(qa, ka_ref[r0:r1, :]), -jnp.inf)
        m = jnp.max(u_d, axis=1, keepdims=True)
        if n > 0:
            u_o = _dot_nt(qa, ka_ref[0:r0, :])
            m = jnp.maximum(m, jnp.max(u_o, axis=1, keepdims=True))
        p_d = jnp.exp2((u_d - m) * c1)
        l = jnp.sum(p_d, axis=1, keepdims=True)
        acc = _dot(p_d.astype(BF16), vb_ref[r0:r1, :])
        if n > 0:
            p_o = jnp.exp2((u_o - m) * c1)
            l = l + jnp.sum(p_o, axis=1, keepdims=True)
            acc = acc + _dot(p_o.astype(BF16), vb_ref[0:r0, :])
        o_ref[r0:r1, :] = (acc / l).astype(o_ref.dtype)


def fox_flash(q, k_stack, v_stack, slot, fcum, *, batch, seq, n_heads, tq):
    hd = FOX_HEAD_DIM
    kern = functools.partial(_fox_flash_kernel, seq=seq, tq=tq)
    return pl.pallas_call(
        kern,
        grid=(batch, n_heads),
        in_specs=[
            pl.BlockSpec((seq, hd), lambda b, h: (b, h)),
            pl.BlockSpec((None, seq, hd), lambda b, h: (slot, b, h)),
            pl.BlockSpec((None, seq, hd), lambda b, h: (slot, b, h)),
            pl.BlockSpec((seq, LANES), lambda b, h: (b, 0)),
        ],
        out_specs=pl.BlockSpec((seq, hd), lambda b, h: (b, h)),
        out_shape=jax.ShapeDtypeStruct((batch * seq, n_heads * hd), BF16),
        scratch_shapes=[pltpu.VMEM((seq, 2 * hd), BF16), pltpu.VMEM((seq, 2 * hd), BF16),
                        pltpu.VMEM((seq, hd), BF16)],
        compiler_params=_cparams(("parallel", "parallel")),
        name="fox_flash",
    )(q, k_stack, v_stack, fcum)


def _past_bias_kernel(pt_ref, u_ref, *rest, pages_per_step):
    pp = pages_per_step
    lf_refs = rest[0:pp]
    o_ref = rest[pp]
    run_ref = rest[pp + 1]

    @pl.when(pl.program_id(1) == 0)
    def _():
        run_ref[...] = jnp.zeros_like(run_ref)

    for p in range(pp):
        lf = lf_refs[p][...]
        su = _dot01(lf, u_ref[...])
        run = run_ref[...]
        o_ref[pp - 1 - p] = run + su[:, :LANES]
        run_ref[...] = run + su[:, LANES:]


def fox_past_bias(cache_logf_t, page_table, layer, *, pages_per_step):
    batch, n_pages = page_table.shape
    n_heads = cache_logf_t.shape[2]
    pp = pages_per_step
    nblk = n_pages // pp
    pos = jnp.arange(PAGE_SIZE)
    u = jnp.concatenate([(pos[:, None] > pos[None, :]), jnp.ones((PAGE_SIZE, PAGE_SIZE), bool)],
                        axis=1).astype(BF16)

    def page_map(p):
        return lambda b, j, pt: (layer, pt[b, n_pages - 1 - (j * pp + p)], 0, 0)

    grid_spec = pltpu.PrefetchScalarGridSpec(
        num_scalar_prefetch=1,
        grid=(batch, nblk),
        in_specs=[pl.BlockSpec((PAGE_SIZE, 2 * PAGE_SIZE), lambda b, j, pt: (0, 0))]
        + [pl.BlockSpec((None, None, n_heads, PAGE_SIZE), page_map(p)) for p in range(pp)],
        out_specs=pl.BlockSpec((None, pp, n_heads, PAGE_SIZE), lambda b, j, pt: (b, nblk - 1 - j, 0, 0)),
        scratch_shapes=[pltpu.VMEM((n_heads, PAGE_SIZE), F32)],
    )
    return pl.pallas_call(
        functools.partial(_past_bias_kernel, pages_per_step=pp),
        grid_spec=grid_spec,
        out_shape=jax.ShapeDtypeStruct((batch, n_pages, n_heads, PAGE_SIZE), F32),
        compiler_params=_cparams(("parallel", "arbitrary")),
        name="fox_past_bias",
    )(page_table, u, *([cache_logf_t] * pp))


def _decode_kernel(pt_ref, q_ref, kn_ref, vn_ref, fcol_ref, frow_ref, trow_ref, *rest, n_heads, pages_per_step,
                   axis, phases=PHASES):
    pp = pages_per_step
    k_refs = rest[0:pp]
    v_refs = rest[pp:2 * pp]
    o_ref = rest[2 * pp]
    m_ref, l_ref, acc_ref, mb_ref, qb_ref = rest[2 * pp + 1:]
    j = pl.program_id(axis)
    c1 = LOG2E * FOX_HEAD_DIM ** -0.5
    bias_scale = FOX_HEAD_DIM ** 0.5

    if "init" in phases:
        pl.when(j == 0)(functools.partial(_decode_init, q_ref, kn_ref, vn_ref, fcol_ref, frow_ref, m_ref, l_ref,
                                          acc_ref, mb_ref, qb_ref, n_heads=n_heads, c1=c1, bias_scale=bias_scale))
    if "main" in phases:
        _decode_pages(trow_ref, k_refs, v_refs, m_ref, l_ref, acc_ref, mb_ref, qb_ref, c1=c1, bias_scale=bias_scale)
    if "final" in phases:
        @pl.when(j == pl.num_programs(axis) - 1)
        def _():
            o_ref[...] = (acc_ref[...] / l_ref[...]).astype(o_ref.dtype)


def _decode_init(q_ref, kn_ref, vn_ref, fcol_ref, frow_ref, m_ref, l_ref, acc_ref, mb_ref, qb_ref, *, n_heads, c1,
                 bias_scale):
    rows, cols = mb_ref.shape
    r = lax.broadcasted_iota(jnp.int32, (rows, cols), 0)
    c = lax.broadcasted_iota(jnp.int32, (rows, cols), 1)
    same_head = (r % n_heads) == (c % n_heads)
    mb_ref[...] = jnp.where(same_head, fcol_ref[...] * bias_scale, -jnp.inf)
    qb = q_ref[...].astype(BF16)
    qb_ref[...] = qb
    s = _dot_nt(qb, kn_ref[...].astype(BF16)) + mb_ref[:, 0:rows] - frow_ref[...] * bias_scale
    rn = lax.broadcasted_iota(jnp.int32, (rows, rows), 0)
    cn = lax.broadcasted_iota(jnp.int32, (rows, rows), 1)
    s = jnp.where((cn // n_heads) <= (rn // n_heads), s, -jnp.inf)
    m_new = jnp.max(s, axis=1, keepdims=True)
    p = jnp.exp2((s - m_new) * c1)
    m_ref[...] = m_new
    l_ref[...] = jnp.sum(p, axis=1, keepdims=True)
    acc_ref[...] = _dot(p.astype(BF16), vn_ref[...].astype(BF16))


def _decode_pages(trow_ref, k_refs, v_refs, m_ref, l_ref, acc_ref, mb_ref, qb_ref, *, c1, bias_scale):
    s_pages = [_dot_nt(qb_ref[...], k_ref[...].astype(BF16)) + mb_ref[...] + trow_ref[pg] * bias_scale
               for pg, k_ref in enumerate(k_refs)]
    m_old = m_ref[...]
    m_new = m_old
    for s in s_pages:
        m_new = jnp.maximum(m_new, jnp.max(s, axis=1, keepdims=True))
    alpha = jnp.exp2((m_old - m_new) * c1)
    l_new = alpha * l_ref[...]
    acc_new = alpha * acc_ref[...]
    for pg, s in enumerate(s_pages):
        p = jnp.exp2((s - m_new) * c1)
        l_new = l_new + jnp.sum(p, axis=1, keepdims=True)
        acc_new = acc_new + _dot(p.astype(BF16), v_refs[pg][...].astype(BF16))
    l_ref[...] = l_new
    acc_ref[...] = acc_new
    m_ref[...] = m_new


def _decode_parts(q, k_new, v_new, fnew, past_bias, cache_k, cache_v, page_table, layer, *, batch, n_new, n_heads,
                  pages_per_step):
    hd = FOX_HEAD_DIM
    d = n_heads * hd
    n_pages = page_table.shape[1]
    pp = pages_per_step
    rows = n_new * n_heads
    cols = PAGE_SIZE * n_heads
    q2 = q.reshape(batch, rows, hd)
    kn2 = k_new.reshape(batch, rows, hd)
    vn2 = v_new.reshape(batch, rows, hd)
    fcol = fnew.reshape(batch, rows, 1)
    frow = fnew.reshape(batch, 1, rows)
    trow = jnp.swapaxes(past_bias, 2, 3).reshape(batch, n_pages, 1, cols)

    def build(step_of, axis):
        seq_map = lambda *ids: (step_of(*ids)[0], 0, 0)

        def page_map(p):
            def index(*ids):
                b, j = step_of(*ids)
                return (layer, ids[-1][b, j * pp + p], 0, 0)
            return index

        in_specs = [
            pl.BlockSpec((None, rows, hd), seq_map),
            pl.BlockSpec((None, rows, hd), seq_map),
            pl.BlockSpec((None, rows, hd), seq_map),
            pl.BlockSpec((None, rows, 1), seq_map),
            pl.BlockSpec((None, 1, rows), seq_map),
            pl.BlockSpec((None, pp, 1, cols), lambda *ids: (*step_of(*ids), 0, 0)),
        ]
        in_specs += [pl.BlockSpec((None, None, cols, hd), page_map(p)) for p in range(pp)]
        in_specs += [pl.BlockSpec((None, None, cols, hd), page_map(p)) for p in range(pp)]
        return types.SimpleNamespace(
            kernel=functools.partial(_decode_kernel, n_heads=n_heads, pages_per_step=pp, axis=axis),
            batch=batch, n_steps=n_pages // pp, page_table=page_table,
            in_specs=in_specs,
            args=(q2, kn2, vn2, fcol, frow, trow, *([cache_k] * pp), *([cache_v] * pp)),
            out_spec=pl.BlockSpec((None, rows, hd), seq_map),
            out_shape=jax.ShapeDtypeStruct((batch, rows, hd), BF16),
            scratch=[
                pltpu.VMEM((rows, 1), F32),
                pltpu.VMEM((rows, 1), F32),
                pltpu.VMEM((rows, hd), F32),
                pltpu.VMEM((rows, cols), F32),
                pltpu.VMEM((rows, hd), BF16),
            ],
            finish=lambda out: out.reshape(batch * n_new, d),
        )

    return build


def fox_decode(*args, **kwargs):
    dec = _decode_parts(*args, **kwargs)(lambda b, j, pt: (b, j), 1)
    grid_spec = pltpu.PrefetchScalarGridSpec(
        num_scalar_prefetch=1, grid=(dec.batch, dec.n_steps), in_specs=dec.in_specs, out_specs=dec.out_spec,
        scratch_shapes=dec.scratch)
    out = pl.pallas_call(
        dec.kernel, grid_spec=grid_spec, out_shape=dec.out_shape,
        compiler_params=_cparams(("parallel", "arbitrary")), name="fox_decode",
    )(dec.page_table, *dec.args)
    return dec.finish(out)


def kernel(x_prompt, x_sample, cache_k, cache_v, cache_logf, state_conv, state_ssm, page_table, ln_mix_pre, ln_mix_post, ln_ffn_pre, ln_ffn_post, ssd_w_in, ssd_conv_w, ssd_conv_b, ssd_dt_bias, ssd_a_log, ssd_d, ssd_norm, ssd_w_out, fox_w_in, fox_b_f, fox_w_out, ffn_w_gate, ffn_w_up, ffn_w_down):
    bp, lp, d = x_prompt.shape
    bs, ls, _ = x_sample.shape
    depth = ln_mix_pre.shape[0]
    n_att, n_pool, page, fox_heads, fox_hd = cache_k.shape
    ssd_heads = ssd_dt_bias.shape[1]
    d_inner = ssd_heads * SSD_HEAD_DIM
    conv_dim = ssd_conv_w.shape[2]
    mp, ms = bp * lp, bs * ls
    tmp, tms = min(mp, 1024), min(ms, 1024)

    hp = x_prompt.reshape(mp, d)
    hs = x_sample.reshape(ms, d)
    ck = cache_k.reshape(n_att, n_pool, page * fox_heads, fox_hd)
    cv = cache_v.reshape(n_att, n_pool, page * fox_heads, fox_hd)
    clf_t = jnp.swapaxes(cache_logf, 2, 3)

    zero_conv = jnp.zeros((bp, SUBLANES, conv_dim), F32)
    zero_ssm = jnp.zeros((1, bp, ssd_heads, SSD_HEAD_DIM, D_STATE), F32)
    conv_init_s = jnp.pad(state_conv, ((0, 0), (0, 0), (SUBLANES - (CONV_W - 1), 0), (0, 0)))
    ls_pad = SSD_CHUNK

    w_ssd_out, w_fox_out = ssd_w_out.astype(BF16), fox_w_out.astype(BF16)

    fp_l, cp_l, sp_l = [], [], []
    fs_l, cs_l, ss_l = [], [], []
    kst_p = vst_p = kst_s = vst_s = None
    gps = SSD_GROUPS_PER_STEP
    ffn_s = lambda h, i: ffn(h, ln_ffn_pre[i], ffn_w_gate, ffn_w_up, ffn_w_down, i, ln_ffn_post[i], tm=tms, tf=512)
    ffn_p = lambda h, i, w: ffn(h, ln_ffn_pre[i], *w, 0, ln_ffn_post[i], tm=min(mp, 512), tf=512)
    for i in range(0, depth, 2):
        j = i // 2
        par = (ssd_conv_w[j], ssd_conv_b[j], ssd_dt_bias[j], ssd_a_log[j], ssd_d[j], ssd_norm[j])
        zx_s, w_ssd_in = norm_matmul(hs, ln_mix_pre[i], ssd_w_in, j, tm=tms, tn=SSD_IN_TN)
        zx_s3 = zx_s.reshape(bs, ls, -1)
        zx_sp = jnp.pad(zx_s3, ((0, 0), (0, ls_pad - ls), (0, 0))).reshape(bs * ls_pad, -1)
        y_sp, st_s = ssd_core(zx_sp, conv_init_s[j], state_ssm, j, *par, batch=bs, seq=ls_pad, valid_rows=ls,
                              groups_per_step=gps)
        y_s = y_sp.reshape(bs, ls_pad, d_inner)[:, :ls].reshape(ms, d_inner)
        hs = matmul_norm_res(y_s, w_ssd_out, j, ln_mix_post[i], hs, tm=tms)
        hs, *w_ffn = ffn_s(hs, i)
        q_s, kst_s, vst_s, fl_s, w_fox_in, w_fox_inf = fox_in_proj(
            hs, ln_mix_pre[i + 1], fox_w_in, fox_w_in, j, kst_s, vst_s, j, n_att, tm=tms, tn=FOX_IN_TN,
            n_heads=fox_heads)
        lf_s, fc_s = fox_forget(fl_s, fox_b_f[j], batch=bs, seq=ls)
        tb_s = fox_past_bias(clf_t, page_table, j, pages_per_step=16)
        decode = _decode_parts(q_s, kst_s[j], vst_s[j], fc_s[:, :fox_heads], tb_s, ck, cv, page_table, j, batch=bs,
                               n_new=ls, n_heads=fox_heads, pages_per_step=8)
        zx_p = norm_matmul(hp, ln_mix_pre[i], w_ssd_in, 0, tm=tmp, tn=SSD_IN_TN)
        y_p, st_p, a_s = ssd_core(zx_p, zero_conv, zero_ssm, 0, *par, batch=bp, seq=lp, valid_rows=SSD_CHUNK,
                                  groups_per_step=gps, decode=decode)
        hp = matmul_norm_res(y_p, w_ssd_out, j, ln_mix_post[i], hp, tm=min(mp, 256))
        hp = ffn_p(hp, i, w_ffn)
        cp_l.append(zx_p.reshape(bp, lp, -1)[:, lp - (CONV_W - 1):, d_inner:d_inner + conv_dim])
        cs_l.append(zx_s3[:, ls - (CONV_W - 1):, d_inner:d_inner + conv_dim])
        sp_l.append(st_p)
        ss_l.append(st_s)
        hs = matmul_norm_res(a_s, w_fox_out, j, ln_mix_post[i + 1], hs, tm=tms)
        hs, *w_ffn = ffn_s(hs, i + 1)
        q_p, kst_p, vst_p, fl_p = fox_in_proj(
            hp, ln_mix_pre[i + 1], w_fox_in, w_fox_inf, 0, kst_p, vst_p, j, n_att, tm=min(mp, 1024), tn=512,
            n_heads=fox_heads)
        lf_p, fc_p = fox_forget(fl_p, fox_b_f[j], batch=bp, seq=lp)
        a_p = fox_flash(q_p, kst_p, vst_p, j, fc_p, batch=bp, seq=lp, n_heads=fox_heads, tq=256)
        hp = matmul_norm_res(a_p, w_fox_out, j, ln_mix_post[i + 1], hp, tm=min(mp, 512))
        hp = ffn_p(hp, i + 1, w_ffn)
        fp_l.append(lf_p.reshape(bp, lp, fox_heads))
        fs_l.append(lf_s.reshape(bs, ls, fox_heads))
    kv_p = lambda a: a.reshape(n_att, bp, lp, fox_heads, fox_hd)
    kv_s = lambda a: a.reshape(n_att, bs, ls, fox_heads, fox_hd)
    return (hp.reshape(bp, lp, d), hs.reshape(bs, ls, d),
            kv_p(kst_p), kv_p(vst_p), jnp.stack(fp_l), jnp.stack(cp_l), jnp.stack(sp_l),
            kv_s(kst_s), kv_s(vst_s), jnp.stack(fs_l), jnp.stack(cs_l), jnp.stack(ss_l))
```

```python
import functools
import types

import jax
import jax.numpy as jnp
from jax import lax
from jax.experimental import pallas as pl
from jax.experimental.pallas import tpu as pltpu

F32 = jnp.float32
BF16 = jnp.bfloat16

RMS_EPS = 1e-6
LOG2E = 1.4426950408889634
LANES = 128
SUBLANES = 8
SSD_HEAD_DIM = 64
SSD_GROUPS = 8
D_STATE = 128
CONV_W = 4
SSD_CHUNK = 128
FOX_HEAD_DIM = 128
PAGE_SIZE = 128
VMEM_LIMIT = 60 * 1024 * 1024
SSD_IN_TN = 1152
FOX_IN_TN = 1024
SSD_GROUPS_PER_STEP = 4
PHASES = ("init", "main", "final")
DECODE_PAGE_GROUP = 4


def _cparams(sem):
    return pltpu.CompilerParams(dimension_semantics=sem, vmem_limit_bytes=VMEM_LIMIT)


def _sigmoid(x):
    return 1.0 / (1.0 + jnp.exp(-x))


def _softplus(x):
    return jnp.maximum(x, 0.0) + jnp.log1p(jnp.exp(-jnp.abs(x)))


def _rms(x, g):
    ms = jnp.mean(x * x, axis=-1, keepdims=True)
    return x * lax.rsqrt(ms + RMS_EPS) * g


def _dot(a, b):
    return jnp.dot(a, b, preferred_element_type=F32)


def _dot_nt(a, b):
    return lax.dot_general(a, b, (((1,), (1,)), ((), ())), preferred_element_type=F32)


def _split3(x):
    hi = x.astype(BF16)
    r1 = x - hi.astype(F32)
    mid = r1.astype(BF16)
    lo = (r1 - mid.astype(F32)).astype(BF16)
    return hi, mid, lo


def _dot01(x, m01):
    hi, mid, lo = _split3(x)
    return (_dot(lo, m01) + _dot(mid, m01)) + _dot(hi, m01)


def _cumsum_rows(x, n_rows):
    row = lax.broadcasted_iota(jnp.int32, x.shape, 0)
    s = 1
    while s < n_rows:
        x = x + jnp.where(row >= s, pltpu.roll(x, s, 0), 0.0)
        s *= 2
    return x


def _norm_matmul_kernel(x_ref, g_ref, w_ref, o_ref, *rest):
    *maybe_wb_ref, xn_ref = rest

    @pl.when(pl.program_id(1) == 0)
    def _():
        xn_ref[...] = _rms(x_ref[...], g_ref[...]).astype(BF16)

    wb = w_ref[...].astype(BF16)
    for wb_ref in maybe_wb_ref:
        wb_ref[...] = wb
    o_ref[...] = _dot(xn_ref[...], wb)


def norm_matmul(x, g, w, layer, *, tm, tn):
    m, d = x.shape
    n = w.shape[2]
    emit_wb = w.dtype != BF16
    assert not emit_wb or m == tm
    out_specs = [pl.BlockSpec((tm, tn), lambda i, j: (i, j))]
    out_shape = [jax.ShapeDtypeStruct((m, n), F32)]
    if emit_wb:
        out_specs.append(pl.BlockSpec((None, d, tn), lambda i, j: (0, 0, j)))
        out_shape.append(jax.ShapeDtypeStruct((1, d, n), BF16))
    res = pl.pallas_call(
        _norm_matmul_kernel,
        grid=(m // tm, pl.cdiv(n, tn)),
        in_specs=[
            pl.BlockSpec((tm, d), lambda i, j: (i, 0)),
            pl.BlockSpec((1, d), lambda i, j: (0, 0)),
            pl.BlockSpec((None, d, tn), lambda i, j: (layer, 0, j)),
        ],
        out_specs=out_specs,
        out_shape=out_shape,
        scratch_shapes=[pltpu.VMEM((tm, d), BF16)],
        compiler_params=_cparams(("parallel", "arbitrary")),
        name="norm_matmul",
    )(x, g.reshape(1, d), w)
    return res if emit_wb else res[0]


def _fox_in_proj_kernel(x_ref, g_ref, w_ref, wf_ref, *rest, tiles_per_part, emit_wb, aliased):
    rest = rest[2:] if aliased else rest
    q_ref, k_ref, v_ref, f_ref = rest[:4]
    xn_ref = rest[-1]
    j = pl.program_id(1)
    wb = w_ref[...].astype(BF16)
    if emit_wb:
        wb_ref, wfb_ref = rest[4:6]
        wb_ref[...] = wb

    @pl.when(j == 0)
    def _():
        xn = _rms(x_ref[...], g_ref[...]).astype(BF16)
        xn_ref[...] = xn
        wfb = wf_ref[...].astype(BF16)
        if emit_wb:
            wfb_ref[...] = wfb
        f_ref[...] = _dot(xn, wfb)

    y = _dot(xn_ref[...], wb)

    @pl.when(j < tiles_per_part)
    def _():
        q_ref[...] = y

    @pl.when((j >= tiles_per_part) & (j < 2 * tiles_per_part))
    def _():
        k_ref[...] = y

    @pl.when(j >= 2 * tiles_per_part)
    def _():
        v_ref[...] = y


def fox_in_proj(x, g, w, w_f, layer, k_stack, v_stack, slot, n_slots, *, tm, tn, n_heads):
    m, d = x.shape
    n = w.shape[2]
    tpp = d // tn
    f_blk = (n - n_heads) // LANES
    emit_wb = w.dtype != BF16
    aliased = k_stack is not None
    assert not emit_wb or m == tm
    part = lambda p: (lambda i, j: (i, jnp.clip(j - p * tpp, 0, tpp - 1)))
    spart = lambda p: (lambda i, j: (slot, i, jnp.clip(j - p * tpp, 0, tpp - 1)))
    in_specs = [
        pl.BlockSpec((tm, d), lambda i, j: (i, 0)),
        pl.BlockSpec((1, d), lambda i, j: (0, 0)),
        pl.BlockSpec((None, d, tn), lambda i, j: (layer, 0, j)),
        pl.BlockSpec((None, d, LANES), lambda i, j: (layer, 0, f_blk)),
    ]
    args = [x, g.reshape(1, d), w, w_f]
    aliases = {}
    if aliased:
        in_specs += [pl.BlockSpec(memory_space=pl.ANY), pl.BlockSpec(memory_space=pl.ANY)]
        args += [k_stack, v_stack]
        aliases = {4: 1, 5: 2}
    out_specs = [
        pl.BlockSpec((tm, tn), part(0)),
        pl.BlockSpec((None, tm, tn), spart(1)),
        pl.BlockSpec((None, tm, tn), spart(2)),
        pl.BlockSpec((tm, LANES), lambda i, j: (i, 0)),
    ]
    out_shape = [
        jax.ShapeDtypeStruct((m, d), F32),
        jax.ShapeDtypeStruct((n_slots, m, d), F32),
        jax.ShapeDtypeStruct((n_slots, m, d), F32),
        jax.ShapeDtypeStruct((m, LANES), F32),
    ]
    if emit_wb:
        out_specs += [pl.BlockSpec((None, d, tn), lambda i, j: (0, 0, j)),
                      pl.BlockSpec((None, d, LANES), lambda i, j: (0, 0, f_blk))]
        out_shape += [jax.ShapeDtypeStruct((1, d, n), BF16), jax.ShapeDtypeStruct((1, d, n), BF16)]
    res = pl.pallas_call(
        functools.partial(_fox_in_proj_kernel, tiles_per_part=tpp, emit_wb=emit_wb, aliased=aliased),
        grid=(m // tm, 3 * tpp),
        in_specs=in_specs,
        out_specs=out_specs,
        out_shape=out_shape,
        input_output_aliases=aliases,
        scratch_shapes=[pltpu.VMEM((tm, d), BF16)],
        compiler_params=_cparams(("arbitrary", "arbitrary")),
        name="fox_in_proj",
    )(*args)
    return res


def _matmul_norm_res_kernel(y_ref, w_ref, g_ref, h_ref, o_ref):
    o_ref[...] = h_ref[...] + _rms(_dot(y_ref[...], w_ref[...]), g_ref[...])


def matmul_norm_res(y, w, layer, g, h, *, tm):
    m, kd = y.shape
    d = w.shape[2]
    return pl.pallas_call(
        _matmul_norm_res_kernel,
        grid=(m // tm,),
        in_specs=[
            pl.BlockSpec((tm, kd), lambda i: (i, 0)),
            pl.BlockSpec((None, kd, d), lambda i: (layer, 0, 0)),
            pl.BlockSpec((1, d), lambda i: (0, 0)),
            pl.BlockSpec((tm, d), lambda i: (i, 0)),
        ],
        out_specs=pl.BlockSpec((tm, d), lambda i: (i, 0)),
        out_shape=jax.ShapeDtypeStruct((m, d), F32),
        compiler_params=_cparams(("parallel",)),
        name="matmul_norm_res",
    )(y, w, g.reshape(1, d), h)


def _ffn_kernel(h_ref, gpre_ref, wg_ref, wu_ref, wd_ref, gpost_ref, o_ref, *rest):
    *maybe_wb_refs, xn_ref, acc_ref = rest
    f = pl.program_id(1)

    @pl.when(f == 0)
    def _():
        xn_ref[...] = _rms(h_ref[...], gpre_ref[...]).astype(BF16)
        acc_ref[...] = jnp.zeros_like(acc_ref)

    wg, wu, wd = (r[...].astype(BF16) for r in (wg_ref, wu_ref, wd_ref))
    for wb_ref, wb in zip(maybe_wb_refs, (wg, wu, wd)):
        wb_ref[...] = wb
    xn = xn_ref[...]
    gate = _dot(xn, wg)
    up = _dot(xn, wu)
    act = (gate * _sigmoid(gate) * up).astype(BF16)
    acc_ref[...] += _dot(act, wd)

    @pl.when(f == pl.num_programs(1) - 1)
    def _():
        o_ref[...] = h_ref[...] + _rms(acc_ref[...], gpost_ref[...])


def ffn(h, g_pre, w_gate, w_up, w_down, layer, g_post, *, tm, tf):
    m, d = h.shape
    dff = w_gate.shape[2]
    emit_wb = w_gate.dtype != BF16
    assert not emit_wb or m == tm
    out_specs = [pl.BlockSpec((tm, d), lambda i, f: (i, 0))]
    out_shape = [jax.ShapeDtypeStruct((m, d), F32)]
    if emit_wb:
        out_specs += [pl.BlockSpec((None, d, tf), lambda i, f: (0, 0, f)),
                      pl.BlockSpec((None, d, tf), lambda i, f: (0, 0, f)),
                      pl.BlockSpec((None, tf, d), lambda i, f: (0, f, 0))]
        out_shape += [jax.ShapeDtypeStruct((1, d, dff), BF16), jax.ShapeDtypeStruct((1, d, dff), BF16),
                      jax.ShapeDtypeStruct((1, dff, d), BF16)]
    res = pl.pallas_call(
        _ffn_kernel,
        grid=(m // tm, dff // tf),
        in_specs=[
            pl.BlockSpec((tm, d), lambda i, f: (i, 0)),
            pl.BlockSpec((1, d), lambda i, f: (0, 0)),
            pl.BlockSpec((None, d, tf), lambda i, f: (layer, 0, f)),
            pl.BlockSpec((None, d, tf), lambda i, f: (layer, 0, f)),
            pl.BlockSpec((None, tf, d), lambda i, f: (layer, f, 0)),
            pl.BlockSpec((1, d), lambda i, f: (0, 0)),
        ],
        out_specs=out_specs,
        out_shape=out_shape,
        scratch_shapes=[pltpu.VMEM((tm, d), BF16), pltpu.VMEM((tm, d), F32)],
        compiler_params=_cparams(("parallel", "arbitrary")),
        name="ffn",
    )(h, g_pre.reshape(1, d), w_gate, w_up, w_down, g_post.reshape(1, d))
    return res if emit_wb else res[0]


def _conv_silu(pad_ref, x, w_ref, b_ref, L):
    pad_ref[SUBLANES:SUBLANES + L, :] = x
    acc = b_ref[...] + x * w_ref[CONV_W - 1:CONV_W, :]
    for k in range(1, CONV_W):
        acc = acc + pad_ref[SUBLANES - k:SUBLANES - k + L, :] * w_ref[CONV_W - 1 - k:CONV_W - k, :]
    pad_ref[0:SUBLANES, :] = pad_ref[L:L + SUBLANES, :]
    return acc * _sigmoid(acc)


def _ssd_kernel(z_ref, x_ref, b_ref, c_ref, dt_ref,
                cwx_ref, cwb_ref, cwc_ref, cbx_ref, cbb_ref, cbc_ref,
                cix_ref, cib_ref, cic_ref,
                dtb_ref, alog_ref, dsk_ref, ng_ref, exp_ref, h0_ref,
                y_ref, hout_ref,
                xpad_ref, bpad_ref, cpad_ref, st_ref, *, L, valid_rows, n_heads, gw, phases=PHASES):
    c = pl.program_id(2)
    width = x_ref.shape[1]
    n_grp = width // gw
    heads = gw // SSD_HEAD_DIM

    if "init" in phases:
        @pl.when(c == 0)
        def _():
            xpad_ref[0:SUBLANES, :] = cix_ref[...]
            bpad_ref[0:SUBLANES, :] = cib_ref[...]
            cpad_ref[0:SUBLANES, :] = cic_ref[...]
            st_ref[...] = h0_ref[...].reshape(width, D_STATE).T

    if "main" in phases:
        _ssd_chunk(z_ref, x_ref, b_ref, c_ref, dt_ref, cwx_ref, cwb_ref, cwc_ref, cbx_ref, cbb_ref, cbc_ref,
                   dtb_ref, alog_ref, dsk_ref, ng_ref, exp_ref, y_ref, xpad_ref, bpad_ref, cpad_ref, st_ref,
                   L=L, valid_rows=valid_rows, n_heads=n_heads, gw=gw)

    if "final" in phases:
        @pl.when(c == pl.num_programs(2) - 1)
        def _():
            hout_ref[...] = st_ref[...].T.reshape(n_grp * heads, SSD_HEAD_DIM, D_STATE)


def _ssd_chunk(z_ref, x_ref, b_ref, c_ref, dt_ref, cwx_ref, cwb_ref, cwc_ref, cbx_ref, cbb_ref, cbc_ref,
               dtb_ref, alog_ref, dsk_ref, ng_ref, exp_ref, y_ref, xpad_ref, bpad_ref, cpad_ref, st_ref,
               *, L, valid_rows, n_heads, gw):
    width = x_ref.shape[1]
    n_grp = width // gw
    heads = gw // SSD_HEAD_DIM
    xs = _conv_silu(xpad_ref, x_ref[...], cwx_ref, cbx_ref, L)
    bm = _conv_silu(bpad_ref, b_ref[...], cwb_ref, cbb_ref, L)
    cm = _conv_silu(cpad_ref, c_ref[...], cwc_ref, cbc_ref, L)

    lane = lax.broadcasted_iota(jnp.int32, (L, LANES), 1)
    row = lax.broadcasted_iota(jnp.int32, (L, LANES), 0)
    dt = _softplus(dt_ref[...] + dtb_ref[...])
    dt = jnp.where((lane < n_heads) & (row < valid_rows), dt, 0.0)
    dta = dt * (-jnp.exp(alog_ref[...]))
    acum = _cumsum_rows(dta, L)
    expand = exp_ref[...]
    dt_w = _dot01(dt, expand)
    acum_w = _dot01(acum, expand)
    a_last = acum_w[L - 1:L, :]

    xdt = xs * dt_w
    xdec = (xdt * jnp.exp(a_last - acum_w)).astype(BF16)
    exp_acum = jnp.exp(acum_w)
    exp_last = jnp.exp(a_last)
    acum_t = acum_w.T
    li = lax.broadcasted_iota(jnp.int32, (L, L), 0)
    si = lax.broadcasted_iota(jnp.int32, (L, L), 1)
    tril = si <= li
    low_half = lax.broadcasted_iota(jnp.int32, (L, LANES), 1) < SSD_HEAD_DIM
    z = z_ref[...]
    gate = z * _sigmoid(z)
    for gi in range(n_grp):
        g0 = gi * gw
        n0 = gi * D_STATE
        bm_g = bm[:, n0:n0 + D_STATE]
        cm_bf = cm[:, n0:n0 + D_STATE].astype(BF16)
        cb = _dot_nt(cm_bf, bm_g.astype(BF16))
        y_pairs = []
        for pr in range(heads // 2):
            p0 = g0 + pr * LANES
            xp = xdt[:, p0:p0 + LANES].astype(BF16)
            y_two = []
            for r0 in (p0, p0 + SSD_HEAD_DIM):
                col = acum_w[:, r0:r0 + 1]
                rowv = acum_t[r0:r0 + 1, :]
                decay = jnp.exp(jnp.where(tril, col - rowv, -jnp.inf))
                y_two.append(_dot((cb * decay).astype(BF16), xp))
            y_pairs.append(jnp.where(low_half, y_two[0], y_two[1]))
        y_diag = jnp.concatenate(y_pairs, axis=1)

        st = st_ref[:, g0:g0 + gw]
        y_off = exp_acum[:, g0:g0 + gw] * _dot(cm_bf, st.astype(BF16))
        st_new = exp_last[:, g0:g0 + gw] * st + _dot(bm_g.T.astype(BF16), xdec[:, g0:g0 + gw])
        st_ref[:, g0:g0 + gw] = st_new

        y = y_diag + y_off + xs[:, g0:g0 + gw] * dsk_ref[:, g0:g0 + gw]
        y = y * gate[:, g0:g0 + gw]
        y_ref[:, g0:g0 + gw] = _rms(y, ng_ref[:, g0:g0 + gw]).astype(y_ref.dtype)


def _ssd_decode_kernel(pt_ref, *refs, ssd_kernel, decode_kernel, n_ssd_in, n_dec_in, n_ssd_scratch):
    ssd_in, dec_in = refs[:n_ssd_in], refs[n_ssd_in:n_ssd_in + n_dec_in]
    rest = refs[n_ssd_in + n_dec_in:]
    ssd_out, dec_out, scr = rest[:2], rest[2:3], rest[3:]
    ssd_refs = (*ssd_in, *ssd_out, *scr[:n_ssd_scratch])
    dec_refs = (pt_ref, *dec_in, *dec_out, *scr[n_ssd_scratch:])
    for phase in PHASES:
        ssd_kernel(*ssd_refs, phases=(phase,))
        decode_kernel(*dec_refs, phases=(phase,))


def ssd_core(zx, conv_init, h0, layer, conv_w, conv_b, dt_bias, a_log, d_skip, norm_g,
             *, batch, seq, valid_rows, groups_per_step, decode=None):
    n_heads = dt_bias.shape[0]
    d_inner = n_heads * SSD_HEAD_DIM
    gw = d_inner // SSD_GROUPS
    gps = groups_per_step
    wx = gps * gw
    wn = gps * D_STATE
    hs = gps * (n_heads // SSD_GROUPS)
    L = min(SSD_CHUNK, seq)
    nc = seq // L
    xb0 = d_inner // wx
    bb0 = (2 * d_inner) // wn
    cb0 = (2 * d_inner + SSD_GROUPS * D_STATE) // wn
    dtb0 = (2 * d_inner + 2 * SSD_GROUPS * D_STATE) // LANES
    cwb0 = d_inner // wn
    cwc0 = (d_inner + SSD_GROUPS * D_STATE) // wn

    pad_h = LANES - n_heads
    dtb = jnp.pad(dt_bias.astype(F32), (0, pad_h)).reshape(1, LANES)
    alog = jnp.pad(a_log.astype(F32), (0, pad_h)).reshape(1, LANES)
    dsk = jnp.repeat(d_skip.astype(F32), SSD_HEAD_DIM).reshape(1, d_inner)
    head_of_lane = jnp.arange(d_inner) // SSD_HEAD_DIM
    expand = (jnp.arange(LANES)[:, None] == head_of_lane[None, :]).astype(BF16)
    conv_b2 = conv_b.reshape(1, -1)

    row_blk = lambda b, g, c, *_: b * nc + c
    kern = functools.partial(_ssd_kernel, L=L, valid_rows=valid_rows, n_heads=n_heads, gw=gw)
    n_gsteps = SSD_GROUPS // gps
    grid = (batch, n_gsteps, nc)
    in_specs = [
            pl.BlockSpec((L, wx), lambda b, g, c, *_: (row_blk(b, g, c), g)),
            pl.BlockSpec((L, wx), lambda b, g, c, *_: (row_blk(b, g, c), xb0 + g)),
            pl.BlockSpec((L, wn), lambda b, g, c, *_: (row_blk(b, g, c), bb0 + g)),
            pl.BlockSpec((L, wn), lambda b, g, c, *_: (row_blk(b, g, c), cb0 + g)),
            pl.BlockSpec((L, LANES), lambda b, g, c, *_: (row_blk(b, g, c), dtb0)),
            pl.BlockSpec((CONV_W, wx), lambda b, g, c, *_: (0, g)),
            pl.BlockSpec((CONV_W, wn), lambda b, g, c, *_: (0, cwb0 + g)),
            pl.BlockSpec((CONV_W, wn), lambda b, g, c, *_: (0, cwc0 + g)),
            pl.BlockSpec((1, wx), lambda b, g, c, *_: (0, g)),
            pl.BlockSpec((1, wn), lambda b, g, c, *_: (0, cwb0 + g)),
            pl.BlockSpec((1, wn), lambda b, g, c, *_: (0, cwc0 + g)),
            pl.BlockSpec((None, SUBLANES, wx), lambda b, g, c, *_: (b, 0, g)),
            pl.BlockSpec((None, SUBLANES, wn), lambda b, g, c, *_: (b, 0, cwb0 + g)),
            pl.BlockSpec((None, SUBLANES, wn), lambda b, g, c, *_: (b, 0, cwc0 + g)),
            pl.BlockSpec((1, LANES), lambda b, g, c, *_: (0, 0)),
            pl.BlockSpec((1, LANES), lambda b, g, c, *_: (0, 0)),
            pl.BlockSpec((1, wx), lambda b, g, c, *_: (0, g)),
            pl.BlockSpec((1, wx), lambda b, g, c, *_: (0, g)),
            pl.BlockSpec((LANES, wx), lambda b, g, c, *_: (0, g)),
            pl.BlockSpec((None, None, hs, SSD_HEAD_DIM, D_STATE), lambda b, g, c, *_: (layer, b, g, 0, 0)),
    ]
    out_specs = [
        pl.BlockSpec((L, wx), lambda b, g, c, *_: (row_blk(b, g, c), g)),
        pl.BlockSpec((None, hs, SSD_HEAD_DIM, D_STATE), lambda b, g, c, *_: (b, g, 0, 0)),
    ]
    out_shape = [
        jax.ShapeDtypeStruct((batch * seq, d_inner), BF16),
        jax.ShapeDtypeStruct((batch, n_heads, SSD_HEAD_DIM, D_STATE), F32),
    ]
    scratch = [
        pltpu.VMEM((L + SUBLANES, wx), F32),
        pltpu.VMEM((L + SUBLANES, wn), F32),
        pltpu.VMEM((L + SUBLANES, wn), F32),
        pltpu.VMEM((D_STATE, wx), F32),
    ]
    args = (zx, zx, zx, zx, zx, conv_w, conv_w, conv_w, conv_b2, conv_b2, conv_b2,
            conv_init, conv_init, conv_init, dtb, alog, dsk, norm_g.reshape(1, d_inner), expand, h0)
    if decode is None:
        y, hout = pl.pallas_call(
            kern, grid=grid, in_specs=in_specs, out_specs=out_specs, out_shape=out_shape, scratch_shapes=scratch,
            compiler_params=_cparams(("parallel", "parallel", "arbitrary")), name="ssd_core",
        )(*args)
        return y, hout
    dec = decode(lambda b, g, c, pt: (b * n_gsteps + g, c), 2)
    assert (dec.batch, dec.n_steps) == (batch * n_gsteps, nc)
    fused = functools.partial(_ssd_decode_kernel, ssd_kernel=kern, decode_kernel=dec.kernel,
                              n_ssd_in=len(in_specs), n_dec_in=len(dec.in_specs), n_ssd_scratch=len(scratch))
    grid_spec = pltpu.PrefetchScalarGridSpec(
        num_scalar_prefetch=1, grid=grid, in_specs=in_specs + dec.in_specs,
        out_specs=out_specs + [dec.out_spec], scratch_shapes=scratch + dec.scratch)
    y, hout, att = pl.pallas_call(
        fused, grid_spec=grid_spec, out_shape=out_shape + [dec.out_shape],
        compiler_params=_cparams(("arbitrary", "arbitrary", "arbitrary")), name="ssd_core_fox_decode",
    )(dec.page_table, *args, *dec.args)
    return y, hout, dec.finish(att)


def _forget_kernel(p_ref, bf_ref, logf_ref, fcum_ref, *, seq, n_heads):
    lane = lax.broadcasted_iota(jnp.int32, (seq, LANES), 1)
    logit = p_ref[...] + bf_ref[...]
    logf = jnp.where(lane < n_heads, -_softplus(-logit), 0.0)
    logf_ref[...] = logf[:, :n_heads]
    fcum_ref[...] = _cumsum_rows(logf, seq)


def fox_forget(flog, b_f, *, batch, seq):
    n_heads = b_f.shape[0]
    bfp = jnp.pad(b_f.astype(F32), (0, LANES - n_heads)).reshape(1, LANES)
    return pl.pallas_call(
        functools.partial(_forget_kernel, seq=seq, n_heads=n_heads),
        grid=(batch,),
        in_specs=[pl.BlockSpec((seq, LANES), lambda b: (b, 0)),
                  pl.BlockSpec((1, LANES), lambda b: (0, 0))],
        out_specs=[pl.BlockSpec((seq, n_heads), lambda b: (b, 0)),
                   pl.BlockSpec((seq, LANES), lambda b: (b, 0))],
        out_shape=[jax.ShapeDtypeStruct((batch * seq, n_heads), F32),
                   jax.ShapeDtypeStruct((batch * seq, LANES), F32)],
        compiler_params=_cparams(("parallel",)),
        name="fox_forget",
    )(flog, bfp)


N_PIECES = 3


def _fox_flash_kernel(q_ref, k_ref, v_ref, f_ref, o_ref, qa_ref, ka_ref, vb_ref, *, seq, tq):
    h = pl.program_id(1)
    hd = FOX_HEAD_DIM
    c1 = LOG2E * hd ** -0.5
    pieces = jnp.concatenate(_split3(f_ref[...] * hd ** 0.5), axis=1)
    r = lax.broadcasted_iota(jnp.int32, (N_PIECES * LANES, LANES), 0)
    c = lax.broadcasted_iota(jnp.int32, (N_PIECES * LANES, LANES), 1)
    is_h = (r % LANES) == h
    sel_q = jnp.where(is_h & (c == r // LANES), 1.0, 0.0).astype(BF16)
    sel_k = jnp.where(is_h & (c == N_PIECES + r // LANES), -1.0, 0.0).astype(BF16)
    lane = lax.broadcasted_iota(jnp.int32, (seq, LANES), 1)
    ones_q = jnp.where((lane >= N_PIECES) & (lane < 2 * N_PIECES), 1.0, 0.0)
    ones_k = jnp.where(lane < N_PIECES, 1.0, 0.0)
    qa_ref[:, 0:hd] = q_ref[...].astype(BF16)
    qa_ref[:, hd:2 * hd] = (_dot(pieces, sel_q) + ones_q).astype(BF16)
    ka_ref[:, 0:hd] = k_ref[...].astype(BF16)
    ka_ref[:, hd:2 * hd] = (_dot(pieces, sel_k) + ones_k).astype(BF16)
    vb_ref[...] = v_ref[...].astype(BF16)

    ti = lax.broadcasted_iota(jnp.int32, (tq, tq), 0)
    si = lax.broadcasted_iota(jnp.int32, (tq, tq), 1)
    for n in range(seq // tq):
        r0, r1 = n * tq, (n + 1) * tq
        qa = qa_ref[r0:r1, :]
        u_d = jnp.where(si <= ti, _dot_nt(qa, ka_ref[r0:r1, :]), -jnp.inf)
        m = jnp.max(u_d, axis=1, keepdims=True)
        if n > 0:
            u_o = _dot_nt(qa, ka_ref[0:r0, :])
            m = jnp.maximum(m, jnp.max(u_o, axis=1, keepdims=True))
        p_d = jnp.exp2((u_d - m) * c1)
        l = jnp.sum(p_d, axis=1, keepdims=True)
        acc = _dot(p_d.astype(BF16), vb_ref[r0:r1, :])
        if n > 0:
            p_o = jnp.exp2((u_o - m) * c1)
            l = l + jnp.sum(p_o, axis=1, keepdims=True)
            acc = acc + _dot(p_o.astype(BF16), vb_ref[0:r0, :])
        o_ref[r0:r1, :] = (acc / l).astype(o_ref.dtype)


def fox_flash(q, k_stack, v_stack, slot, fcum, *, batch, seq, n_heads, tq):
    hd = FOX_HEAD_DIM
    kern = functools.partial(_fox_flash_kernel, seq=seq, tq=tq)
    return pl.pallas_call(
        kern,
        grid=(batch, n_heads),
        in_specs=[
            pl.BlockSpec((seq, hd), lambda b, h: (b, h)),
            pl.BlockSpec((None, seq, hd), lambda b, h: (slot, b, h)),
            pl.BlockSpec((None, seq, hd), lambda b, h: (slot, b, h)),
            pl.BlockSpec((seq, LANES), lambda b, h: (b, 0)),
        ],
        out_specs=pl.BlockSpec((seq, hd), lambda b, h: (b, h)),
        out_shape=jax.ShapeDtypeStruct((batch * seq, n_heads * hd), BF16),
        scratch_shapes=[pltpu.VMEM((seq, 2 * hd), BF16), pltpu.VMEM((seq, 2 * hd), BF16),
                        pltpu.VMEM((seq, hd), BF16)],
        compiler_params=_cparams(("parallel", "parallel")),
        name="fox_flash",
    )(q, k_stack, v_stack, fcum)


def _past_bias_kernel(pt_ref, u_ref, *rest, pages_per_step):
    pp = pages_per_step
    lf_refs = rest[0:pp]
    o_ref = rest[pp]
    run_ref = rest[pp + 1]

    @pl.when(pl.program_id(1) == 0)
    def _():
        run_ref[...] = jnp.zeros_like(run_ref)

    for p in range(pp):
        lf = lf_refs[p][...]
        su = _dot01(lf, u_ref[...])
        run = run_ref[...]
        o_ref[pp - 1 - p] = run + su[:, :LANES]
        run_ref[...] = run + su[:, LANES:]


def fox_past_bias(cache_logf_t, page_table, layer, *, pages_per_step):
    batch, n_pages = page_table.shape
    n_heads = cache_logf_t.shape[2]
    pp = pages_per_step
    nblk = n_pages // pp
    pos = jnp.arange(PAGE_SIZE)
    u = jnp.concatenate([(pos[:, None] > pos[None, :]), jnp.ones((PAGE_SIZE, PAGE_SIZE), bool)],
                        axis=1).astype(BF16)

    def page_map(p):
        return lambda b, j, pt: (layer, pt[b, n_pages - 1 - (j * pp + p)], 0, 0)

    grid_spec = pltpu.PrefetchScalarGridSpec(
        num_scalar_prefetch=1,
        grid=(batch, nblk),
        in_specs=[pl.BlockSpec((PAGE_SIZE, 2 * PAGE_SIZE), lambda b, j, pt: (0, 0))]
        + [pl.BlockSpec((None, None, n_heads, PAGE_SIZE), page_map(p)) for p in range(pp)],
        out_specs=pl.BlockSpec((None, pp, n_heads, PAGE_SIZE), lambda b, j, pt: (b, nblk - 1 - j, 0, 0)),
        scratch_shapes=[pltpu.VMEM((n_heads, PAGE_SIZE), F32)],
    )
    return pl.pallas_call(
        functools.partial(_past_bias_kernel, pages_per_step=pp),
        grid_spec=grid_spec,
        out_shape=jax.ShapeDtypeStruct((batch, n_pages, n_heads, PAGE_SIZE), F32),
        compiler_params=_cparams(("parallel", "arbitrary")),
        name="fox_past_bias",
    )(page_table, u, *([cache_logf_t] * pp))


def _decode_kernel(pt_ref, q_ref, kn_ref, vn_ref, fcol_ref, frow_ref, trow_ref, *rest, n_heads, pages_per_step,
                   axis, phases=PHASES):
    pp = pages_per_step
    k_refs = rest[0:pp]
    v_refs = rest[pp:2 * pp]
    o_ref = rest[2 * pp]
    m_ref, l_ref, acc_ref, mb_ref, qb_ref = rest[2 * pp + 1:]
    j = pl.program_id(axis)
    c1 = LOG2E * FOX_HEAD_DIM ** -0.5
    bias_scale = FOX_HEAD_DIM ** 0.5

    if "init" in phases:
        pl.when(j == 0)(functools.partial(_decode_init, q_ref, kn_ref, vn_ref, fcol_ref, frow_ref, m_ref, l_ref,
                                          acc_ref, mb_ref, qb_ref, n_heads=n_heads, c1=c1, bias_scale=bias_scale))
    if "main" in phases:
        _decode_pages(trow_ref, k_refs, v_refs, m_ref, l_ref, acc_ref, mb_ref, qb_ref, c1=c1, bias_scale=bias_scale)
    if "final" in phases:
        @pl.when(j == pl.num_programs(axis) - 1)
        def _():
            o_ref[...] = (acc_ref[...] / l_ref[...]).astype(o_ref.dtype)


def _decode_init(q_ref, kn_ref, vn_ref, fcol_ref, frow_ref, m_ref, l_ref, acc_ref, mb_ref, qb_ref, *, n_heads, c1,
                 bias_scale):
    rows, cols = mb_ref.shape
    r = lax.broadcasted_iota(jnp.int32, (rows, cols), 0)
    c = lax.broadcasted_iota(jnp.int32, (rows, cols), 1)
    same_head = (r % n_heads) == (c % n_heads)
    mb_ref[...] = jnp.where(same_head, fcol_ref[...] * bias_scale, -jnp.inf)
    qb = q_ref[...].astype(BF16)
    qb_ref[...] = qb
    s = _dot_nt(qb, kn_ref[...].astype(BF16)) + mb_ref[:, 0:rows] - frow_ref[...] * bias_scale
    rn = lax.broadcasted_iota(jnp.int32, (rows, rows), 0)
    cn = lax.broadcasted_iota(jnp.int32, (rows, rows), 1)
    s = jnp.where((cn // n_heads) <= (rn // n_heads), s, -jnp.inf)
    m_new = jnp.max(s, axis=1, keepdims=True)
    p = jnp.exp2((s - m_new) * c1)
    m_ref[...] = m_new
    l_ref[...] = jnp.sum(p, axis=1, keepdims=True)
    acc_ref[...] = _dot(p.astype(BF16), vn_ref[...].astype(BF16))


def _decode_pages(trow_ref, k_refs, v_refs, m_ref, l_ref, acc_ref, mb_ref, qb_ref, *, c1, bias_scale):
    for g0 in range(0, len(k_refs), DECODE_PAGE_GROUP):
        pages = range(g0, min(g0 + DECODE_PAGE_GROUP, len(k_refs)))
        s_pages = [_dot_nt(qb_ref[...], k_refs[pg][...].astype(BF16)) + mb_ref[...] + trow_ref[pg] * bias_scale
                   for pg in pages]
        m_old = m_ref[...]
        m_new = m_old
        for s in s_pages:
            m_new = jnp.maximum(m_new, jnp.max(s, axis=1, keepdims=True))
        alpha = jnp.exp2((m_old - m_new) * c1)
        l_new = alpha * l_ref[...]
        acc_new = alpha * acc_ref[...]
        for pg, s in zip(pages, s_pages):
            p = jnp.exp2((s - m_new) * c1)
            l_new = l_new + jnp.sum(p, axis=1, keepdims=True)
            acc_new = acc_new + _dot(p.astype(BF16), v_refs[pg][...].astype(BF16))
        l_ref[...] = l_new
        acc_ref[...] = acc_new
        m_ref[...] = m_new


def _decode_parts(q, k_new, v_new, fnew, past_bias, cache_k, cache_v, page_table, layer, *, batch, n_new, n_heads,
                  pages_per_step):
    hd = FOX_HEAD_DIM
    d = n_heads * hd
    n_pages = page_table.shape[1]
    pp = pages_per_step
    rows = n_new * n_heads
    cols = PAGE_SIZE * n_heads
    q2 = q.reshape(batch, rows, hd)
    kn2 = k_new.reshape(batch, rows, hd)
    vn2 = v_new.reshape(batch, rows, hd)
    fcol = fnew.reshape(batch, rows, 1)
    frow = fnew.reshape(batch, 1, rows)
    trow = jnp.swapaxes(past_bias, 2, 3).reshape(batch, n_pages, 1, cols)

    def build(step_of, axis):
        seq_map = lambda *ids: (step_of(*ids)[0], 0, 0)

        def page_map(p):
            def index(*ids):
                b, j = step_of(*ids)
                return (layer, ids[-1][b, j * pp + p], 0, 0)
            return index

        in_specs = [
            pl.BlockSpec((None, rows, hd), seq_map),
            pl.BlockSpec((None, rows, hd), seq_map),
            pl.BlockSpec((None, rows, hd), seq_map),
            pl.BlockSpec((None, rows, 1), seq_map),
            pl.BlockSpec((None, 1, rows), seq_map),
            pl.BlockSpec((None, pp, 1, cols), lambda *ids: (*step_of(*ids), 0, 0)),
        ]
        in_specs += [pl.BlockSpec((None, None, cols, hd), page_map(p)) for p in range(pp)]
        in_specs += [pl.BlockSpec((None, None, cols, hd), page_map(p)) for p in range(pp)]
        return types.SimpleNamespace(
            kernel=functools.partial(_decode_kernel, n_heads=n_heads, pages_per_step=pp, axis=axis),
            batch=batch, n_steps=n_pages // pp, page_table=page_table,
            in_specs=in_specs,
            args=(q2, kn2, vn2, fcol, frow, trow, *([cache_k] * pp), *([cache_v] * pp)),
            out_spec=pl.BlockSpec((None, rows, hd), seq_map),
            out_shape=jax.ShapeDtypeStruct((batch, rows, hd), BF16),
            scratch=[
                pltpu.VMEM((rows, 1), F32),
                pltpu.VMEM((rows, 1), F32),
                pltpu.VMEM((rows, hd), F32),
                pltpu.VMEM((rows, cols), F32),
                pltpu.VMEM((rows, hd), BF16),
            ],
            finish=lambda out: out.reshape(batch * n_new, d),
        )

    return build


def fox_decode(*args, **kwargs):
    dec = _decode_parts(*args, **kwargs)(lambda b, j, pt: (b, j), 1)
    grid_spec = pltpu.PrefetchScalarGridSpec(
        num_scalar_prefetch=1, grid=(dec.batch, dec.n_steps), in_specs=dec.in_specs, out_specs=dec.out_spec,
        scratch_shapes=dec.scratch)
    out = pl.pallas_call(
        dec.kernel, grid_spec=grid_spec, out_shape=dec.out_shape,
        compiler_params=_cparams(("parallel", "arbitrary")), name="fox_decode",
    )(dec.page_table, *dec.args)
    return dec.finish(out)


def kernel(x_prompt, x_sample, cache_k, cache_v, cache_logf, state_conv, state_ssm, page_table, ln_mix_pre, ln_mix_post, ln_ffn_pre, ln_ffn_post, ssd_w_in, ssd_conv_w, ssd_conv_b, ssd_dt_bias, ssd_a_log, ssd_d, ssd_norm, ssd_w_out, fox_w_in, fox_b_f, fox_w_out, ffn_w_gate, ffn_w_up, ffn_w_down):
    bp, lp, d = x_prompt.shape
    bs, ls, _ = x_sample.shape
    depth = ln_mix_pre.shape[0]
    n_att, n_pool, page, fox_heads, fox_hd = cache_k.shape
    ssd_heads = ssd_dt_bias.shape[1]
    d_inner = ssd_heads * SSD_HEAD_DIM
    conv_dim = ssd_conv_w.shape[2]
    mp, ms = bp * lp, bs * ls
    tmp, tms = min(mp, 1024), min(ms, 1024)

    hp = x_prompt.reshape(mp, d)
    hs = x_sample.reshape(ms, d)
    ck = cache_k.reshape(n_att, n_pool, page * fox_heads, fox_hd)
    cv = cache_v.reshape(n_att, n_pool, page * fox_heads, fox_hd)
    clf_t = jnp.swapaxes(cache_logf, 2, 3)

    zero_conv = jnp.zeros((bp, SUBLANES, conv_dim), F32)
    zero_ssm = jnp.zeros((1, bp, ssd_heads, SSD_HEAD_DIM, D_STATE), F32)
    conv_init_s = jnp.pad(state_conv, ((0, 0), (0, 0), (SUBLANES - (CONV_W - 1), 0), (0, 0)))
    ls_pad = SSD_CHUNK

    w_ssd_out, w_fox_out = ssd_w_out.astype(BF16), fox_w_out.astype(BF16)

    fp_l, cp_l, sp_l = [], [], []
    fs_l, cs_l, ss_l = [], [], []
    kst_p = vst_p = kst_s = vst_s = None
    gps = SSD_GROUPS_PER_STEP
    ffn_s = lambda h, i: ffn(h, ln_ffn_pre[i], ffn_w_gate, ffn_w_up, ffn_w_down, i, ln_ffn_post[i], tm=tms, tf=512)
    ffn_p = lambda h, i, w: ffn(h, ln_ffn_pre[i], *w, 0, ln_ffn_post[i], tm=min(mp, 512), tf=512)
    for i in range(0, depth, 2):
        j = i // 2
        par = (ssd_conv_w[j], ssd_conv_b[j], ssd_dt_bias[j], ssd_a_log[j], ssd_d[j], ssd_norm[j])
        zx_s, w_ssd_in = norm_matmul(hs, ln_mix_pre[i], ssd_w_in, j, tm=tms, tn=SSD_IN_TN)
        zx_s3 = zx_s.reshape(bs, ls, -1)
        zx_sp = jnp.pad(zx_s3, ((0, 0), (0, ls_pad - ls), (0, 0))).reshape(bs * ls_pad, -1)
        y_sp, st_s = ssd_core(zx_sp, conv_init_s[j], state_ssm, j, *par, batch=bs, seq=ls_pad, valid_rows=ls,
                              groups_per_step=SSD_GROUPS)
        y_s = y_sp.reshape(bs, ls_pad, d_inner)[:, :ls].reshape(ms, d_inner)
        hs = matmul_norm_res(y_s, w_ssd_out, j, ln_mix_post[i], hs, tm=tms)
        hs, *w_ffn = ffn_s(hs, i)
        q_s, kst_s, vst_s, fl_s, w_fox_in, w_fox_inf = fox_in_proj(
            hs, ln_mix_pre[i + 1], fox_w_in, fox_w_in, j, kst_s, vst_s, j, n_att, tm=tms, tn=FOX_IN_TN,
            n_heads=fox_heads)
        lf_s, fc_s = fox_forget(fl_s, fox_b_f[j], batch=bs, seq=ls)
        tb_s = fox_past_bias(clf_t, page_table, j, pages_per_step=32)
        decode = _decode_parts(q_s, kst_s[j], vst_s[j], fc_s[:, :fox_heads], tb_s, ck, cv, page_table, j, batch=bs,
                               n_new=ls, n_heads=fox_heads, pages_per_step=8)
        zx_p = norm_matmul(hp, ln_mix_pre[i], w_ssd_in, 0, tm=tmp, tn=SSD_IN_TN)
        y_p, st_p, a_s = ssd_core(zx_p, zero_conv, zero_ssm, 0, *par, batch=bp, seq=lp, valid_rows=SSD_CHUNK,
                                  groups_per_step=gps, decode=decode)
        hp = matmul_norm_res(y_p, w_ssd_out, j, ln_mix_post[i], hp, tm=min(mp, 256))
        hp = ffn_p(hp, i, w_ffn)
        cp_l.append(zx_p.reshape(bp, lp, -1)[:, lp - (CONV_W - 1):, d_inner:d_inner + conv_dim])
        cs_l.append(zx_s3[:, ls - (CONV_W - 1):, d_inner:d_inner + conv_dim])
        sp_l.append(st_p)
        ss_l.append(st_s)
        hs = matmul_norm_res(a_s, w_fox_out, j, ln_mix_post[i + 1], hs, tm=tms)
        hs, *w_ffn = ffn_s(hs, i + 1)
        q_p, kst_p, vst_p, fl_p = fox_in_proj(
            hp, ln_mix_pre[i + 1], w_fox_in, w_fox_inf, 0, kst_p, vst_p, j, n_att, tm=min(mp, 1024), tn=512,
            n_heads=fox_heads)
        lf_p, fc_p = fox_forget(fl_p, fox_b_f[j], batch=bp, seq=lp)
        a_p = fox_flash(q_p, kst_p, vst_p, j, fc_p, batch=bp, seq=lp, n_heads=fox_heads, tq=256)
        hp = matmul_norm_res(a_p, w_fox_out, j, ln_mix_post[i + 1], hp, tm=min(mp, 512))
        hp = ffn_p(hp, i + 1, w_ffn)
        fp_l.append(lf_p.reshape(bp, lp, fox_heads))
        fs_l.append(lf_s.reshape(bs, ls, fox_heads))
    kv_p = lambda a: a.reshape(n_att, bp, lp, fox_heads, fox_hd)
    kv_s = lambda a: a.reshape(n_att, bs, ls, fox_heads, fox_hd)
    return (hp.reshape(bp, lp, d), hs.reshape(bs, ls, d),
            kv_p(kst_p), kv_p(vst_p), jnp.stack(fp_l), jnp.stack(cp_l), jnp.stack(sp_l),
            kv_s(kst_s), kv_s(vst_s), jnp.stack(fs_l), jnp.stack(cs_l), jnp.stack(ss_l))
```

```python
import functools
import types

import jax
import jax.numpy as jnp
from jax import lax
from jax.experimental import pallas as pl
from jax.experimental.pallas import tpu as pltpu

F32 = jnp.float32
BF16 = jnp.bfloat16

RMS_EPS = 1e-6
LOG2E = 1.4426950408889634
LANES = 128
SUBLANES = 8
SSD_HEAD_DIM = 64
SSD_GROUPS = 8
D_STATE = 128
CONV_W = 4
SSD_CHUNK = 128
FOX_HEAD_DIM = 128
PAGE_SIZE = 128
VMEM_LIMIT = 60 * 1024 * 1024
SSD_IN_TN = 1152
FOX_IN_TN = 1024
SSD_GROUPS_PER_STEP = 2
PHASES = ("init", "main", "final")
DECODE_PAGE_GROUP = 4


def _cparams(sem, flags=None):
    return pltpu.CompilerParams(dimension_semantics=sem, vmem_limit_bytes=VMEM_LIMIT, flags=flags)


def _sigmoid(x):
    return 1.0 / (1.0 + jnp.exp(-x))


def _softplus(x):
    return jnp.maximum(x, 0.0) + jnp.log1p(jnp.exp(-jnp.abs(x)))


def _rms(x, g):
    ms = jnp.mean(x * x, axis=-1, keepdims=True)
    return x * lax.rsqrt(ms + RMS_EPS) * g


def _dot(a, b):
    return jnp.dot(a, b, preferred_element_type=F32)


def _dot_nt(a, b):
    return lax.dot_general(a, b, (((1,), (1,)), ((), ())), preferred_element_type=F32)


def _split3(x):
    hi = x.astype(BF16)
    r1 = x - hi.astype(F32)
    mid = r1.astype(BF16)
    lo = (r1 - mid.astype(F32)).astype(BF16)
    return hi, mid, lo


def _dot01_pieces(pieces, m01):
    hi, mid, lo = pieces
    return (_dot(lo, m01) + _dot(mid, m01)) + _dot(hi, m01)


def _dot01(x, m01):
    return _dot01_pieces(_split3(x), m01)


def _cumsum_rows(x, n_rows):
    row = lax.broadcasted_iota(jnp.int32, x.shape, 0)
    s = 1
    while s < n_rows:
        x = x + jnp.where(row >= s, pltpu.roll(x, s, 0), 0.0)
        s *= 2
    return x


def _norm_matmul_kernel(x_ref, g_ref, w_ref, o_ref, *rest):
    *maybe_wb_ref, xn_ref = rest

    @pl.when(pl.program_id(1) == 0)
    def _():
        xn_ref[...] = _rms(x_ref[...], g_ref[...]).astype(BF16)

    wb = w_ref[...].astype(BF16)
    for wb_ref in maybe_wb_ref:
        wb_ref[...] = wb
    o_ref[...] = _dot(xn_ref[...], wb)


def norm_matmul(x, g, w, layer, *, tm, tn):
    m, d = x.shape
    n = w.shape[2]
    emit_wb = w.dtype != BF16
    assert not emit_wb or m == tm
    out_specs = [pl.BlockSpec((tm, tn), lambda i, j: (i, j))]
    out_shape = [jax.ShapeDtypeStruct((m, n), F32)]
    if emit_wb:
        out_specs.append(pl.BlockSpec((None, d, tn), lambda i, j: (0, 0, j)))
        out_shape.append(jax.ShapeDtypeStruct((1, d, n), BF16))
    res = pl.pallas_call(
        _norm_matmul_kernel,
        grid=(m // tm, pl.cdiv(n, tn)),
        in_specs=[
            pl.BlockSpec((tm, d), lambda i, j: (i, 0)),
            pl.BlockSpec((1, d), lambda i, j: (0, 0)),
            pl.BlockSpec((None, d, tn), lambda i, j: (layer, 0, j)),
        ],
        out_specs=out_specs,
        out_shape=out_shape,
        scratch_shapes=[pltpu.VMEM((tm, d), BF16)],
        compiler_params=_cparams(("parallel", "arbitrary")),
        name="norm_matmul",
    )(x, g.reshape(1, d), w)
    return res if emit_wb else res[0]


def _fox_in_proj_kernel(x_ref, g_ref, w_ref, wf_ref, *rest, tiles_per_part, emit_wb, aliased):
    rest = rest[2:] if aliased else rest
    q_ref, k_ref, v_ref, f_ref = rest[:4]
    xn_ref = rest[-1]
    j = pl.program_id(1)
    wb = w_ref[...].astype(BF16)
    if emit_wb:
        wb_ref, wfb_ref = rest[4:6]
        wb_ref[...] = wb

    @pl.when(j == 0)
    def _():
        xn = _rms(x_ref[...], g_ref[...]).astype(BF16)
        xn_ref[...] = xn
        wfb = wf_ref[...].astype(BF16)
        if emit_wb:
            wfb_ref[...] = wfb
        f_ref[...] = _dot(xn, wfb)

    y = _dot(xn_ref[...], wb)

    @pl.when(j < tiles_per_part)
    def _():
        q_ref[...] = y

    @pl.when((j >= tiles_per_part) & (j < 2 * tiles_per_part))
    def _():
        k_ref[...] = y

    @pl.when(j >= 2 * tiles_per_part)
    def _():
        v_ref[...] = y


def fox_in_proj(x, g, w, w_f, layer, k_stack, v_stack, slot, n_slots, *, tm, tn, n_heads):
    m, d = x.shape
    n = w.shape[2]
    tpp = d // tn
    f_blk = (n - n_heads) // LANES
    emit_wb = w.dtype != BF16
    aliased = k_stack is not None
    assert not emit_wb or m == tm
    part = lambda p: (lambda i, j: (i, jnp.clip(j - p * tpp, 0, tpp - 1)))
    spart = lambda p: (lambda i, j: (slot, i, jnp.clip(j - p * tpp, 0, tpp - 1)))
    in_specs = [
        pl.BlockSpec((tm, d), lambda i, j: (i, 0)),
        pl.BlockSpec((1, d), lambda i, j: (0, 0)),
        pl.BlockSpec((None, d, tn), lambda i, j: (layer, 0, j)),
        pl.BlockSpec((None, d, LANES), lambda i, j: (layer, 0, f_blk)),
    ]
    args = [x, g.reshape(1, d), w, w_f]
    aliases = {}
    if aliased:
        in_specs += [pl.BlockSpec(memory_space=pl.ANY), pl.BlockSpec(memory_space=pl.ANY)]
        args += [k_stack, v_stack]
        aliases = {4: 1, 5: 2}
    out_specs = [
        pl.BlockSpec((tm, tn), part(0)),
        pl.BlockSpec((None, tm, tn), spart(1)),
        pl.BlockSpec((None, tm, tn), spart(2)),
        pl.BlockSpec((tm, LANES), lambda i, j: (i, 0)),
    ]
    out_shape = [
        jax.ShapeDtypeStruct((m, d), F32),
        jax.ShapeDtypeStruct((n_slots, m, d), F32),
        jax.ShapeDtypeStruct((n_slots, m, d), F32),
        jax.ShapeDtypeStruct((m, LANES), F32),
    ]
    if emit_wb:
        out_specs += [pl.BlockSpec((None, d, tn), lambda i, j: (0, 0, j)),
                      pl.BlockSpec((None, d, LANES), lambda i, j: (0, 0, f_blk))]
        out_shape += [jax.ShapeDtypeStruct((1, d, n), BF16), jax.ShapeDtypeStruct((1, d, n), BF16)]
    res = pl.pallas_call(
        functools.partial(_fox_in_proj_kernel, tiles_per_part=tpp, emit_wb=emit_wb, aliased=aliased),
        grid=(m // tm, 3 * tpp),
        in_specs=in_specs,
        out_specs=out_specs,
        out_shape=out_shape,
        input_output_aliases=aliases,
        scratch_shapes=[pltpu.VMEM((tm, d), BF16)],
        compiler_params=_cparams(("arbitrary", "arbitrary")),
        name="fox_in_proj",
    )(*args)
    return res


def _matmul_norm_res_kernel(y_ref, w_ref, g_ref, h_ref, o_ref):
    o_ref[...] = h_ref[...] + _rms(_dot(y_ref[...], w_ref[...]), g_ref[...])


def matmul_norm_res(y, w, layer, g, h, *, tm):
    m, kd = y.shape
    d = w.shape[2]
    return pl.pallas_call(
        _matmul_norm_res_kernel,
        grid=(m // tm,),
        in_specs=[
            pl.BlockSpec((tm, kd), lambda i: (i, 0)),
            pl.BlockSpec((None, kd, d), lambda i: (layer, 0, 0)),
            pl.BlockSpec((1, d), lambda i: (0, 0)),
            pl.BlockSpec((tm, d), lambda i: (i, 0)),
        ],
        out_specs=pl.BlockSpec((tm, d), lambda i: (i, 0)),
        out_shape=jax.ShapeDtypeStruct((m, d), F32),
        compiler_params=_cparams(("parallel",)),
        name="matmul_norm_res",
    )(y, w, g.reshape(1, d), h)


def _ffn_kernel(h_ref, gpre_ref, wg_ref, wu_ref, wd_ref, gpost_ref, o_ref, *rest):
    *maybe_wb_refs, xn_ref, acc_ref = rest
    f = pl.program_id(1)

    @pl.when(f == 0)
    def _():
        xn_ref[...] = _rms(h_ref[...], gpre_ref[...]).astype(BF16)
        acc_ref[...] = jnp.zeros_like(acc_ref)

    wg, wu, wd = (r[...].astype(BF16) for r in (wg_ref, wu_ref, wd_ref))
    for wb_ref, wb in zip(maybe_wb_refs, (wg, wu, wd)):
        wb_ref[...] = wb
    xn = xn_ref[...]
    gate = _dot(xn, wg)
    up = _dot(xn, wu)
    act = (gate * _sigmoid(gate) * up).astype(BF16)
    acc_ref[...] += _dot(act, wd)

    @pl.when(f == pl.num_programs(1) - 1)
    def _():
        o_ref[...] = h_ref[...] + _rms(acc_ref[...], gpost_ref[...])


def ffn(h, g_pre, w_gate, w_up, w_down, layer, g_post, *, tm, tf):
    m, d = h.shape
    dff = w_gate.shape[2]
    emit_wb = w_gate.dtype != BF16
    assert not emit_wb or m == tm
    out_specs = [pl.BlockSpec((tm, d), lambda i, f: (i, 0))]
    out_shape = [jax.ShapeDtypeStruct((m, d), F32)]
    if emit_wb:
        out_specs += [pl.BlockSpec((None, d, tf), lambda i, f: (0, 0, f)),
                      pl.BlockSpec((None, d, tf), lambda i, f: (0, 0, f)),
                      pl.BlockSpec((None, tf, d), lambda i, f: (0, f, 0))]
        out_shape += [jax.ShapeDtypeStruct((1, d, dff), BF16), jax.ShapeDtypeStruct((1, d, dff), BF16),
                      jax.ShapeDtypeStruct((1, dff, d), BF16)]
    res = pl.pallas_call(
        _ffn_kernel,
        grid=(m // tm, dff // tf),
        in_specs=[
            pl.BlockSpec((tm, d), lambda i, f: (i, 0)),
            pl.BlockSpec((1, d), lambda i, f: (0, 0)),
            pl.BlockSpec((None, d, tf), lambda i, f: (layer, 0, f)),
            pl.BlockSpec((None, d, tf), lambda i, f: (layer, 0, f)),
            pl.BlockSpec((None, tf, d), lambda i, f: (layer, f, 0)),
            pl.BlockSpec((1, d), lambda i, f: (0, 0)),
        ],
        out_specs=out_specs,
        out_shape=out_shape,
        scratch_shapes=[pltpu.VMEM((tm, d), BF16), pltpu.VMEM((tm, d), F32)],
        compiler_params=_cparams(("parallel", "arbitrary")),
        name="ffn",
    )(h, g_pre.reshape(1, d), w_gate, w_up, w_down, g_post.reshape(1, d))
    return res if emit_wb else res[0]


def _conv_silu(pad_ref, x_ref, w_ref, b_ref, L, cols):
    x = x_ref[:, cols]
    pad_ref[SUBLANES:SUBLANES + L, cols] = x
    acc = b_ref[:, cols] + x * w_ref[CONV_W - 1:CONV_W, cols]
    for k in range(1, CONV_W):
        acc = acc + pad_ref[SUBLANES - k:SUBLANES - k + L, cols] * w_ref[CONV_W - 1 - k:CONV_W - k, cols]
    pad_ref[0:SUBLANES, cols] = pad_ref[L:L + SUBLANES, cols]
    return acc * _sigmoid(acc)


def _ssd_kernel(z_ref, x_ref, b_ref, c_ref, dt_ref,
                cwx_ref, cwb_ref, cwc_ref, cbx_ref, cbb_ref, cbc_ref,
                cix_ref, cib_ref, cic_ref,
                dtb_ref, alog_ref, dsk_ref, ng_ref, exp_ref, h0_ref,
                y_ref, hout_ref,
                xpad_ref, bpad_ref, cpad_ref, st_ref, *, L, valid_rows, n_heads, gw, phases=PHASES):
    c = pl.program_id(2)
    width = x_ref.shape[1]
    n_grp = width // gw
    heads = gw // SSD_HEAD_DIM

    if "init" in phases:
        @pl.when(c == 0)
        def _():
            xpad_ref[0:SUBLANES, :] = cix_ref[...]
            bpad_ref[0:SUBLANES, :] = cib_ref[...]
            cpad_ref[0:SUBLANES, :] = cic_ref[...]
            st_ref[...] = h0_ref[...].reshape(width, D_STATE).T

    if "main" in phases:
        _ssd_chunk(z_ref, x_ref, b_ref, c_ref, dt_ref, cwx_ref, cwb_ref, cwc_ref, cbx_ref, cbb_ref, cbc_ref,
                   dtb_ref, alog_ref, dsk_ref, ng_ref, exp_ref, y_ref, xpad_ref, bpad_ref, cpad_ref, st_ref,
                   L=L, valid_rows=valid_rows, n_heads=n_heads, gw=gw)

    if "final" in phases:
        @pl.when(c == pl.num_programs(2) - 1)
        def _():
            hout_ref[...] = st_ref[...].T.reshape(n_grp * heads, SSD_HEAD_DIM, D_STATE)


def _ssd_chunk(z_ref, x_ref, b_ref, c_ref, dt_ref, cwx_ref, cwb_ref, cwc_ref, cbx_ref, cbb_ref, cbc_ref,
               dtb_ref, alog_ref, dsk_ref, ng_ref, exp_ref, y_ref, xpad_ref, bpad_ref, cpad_ref, st_ref,
               *, L, valid_rows, n_heads, gw):
    width = x_ref.shape[1]
    n_grp = width // gw
    heads = gw // SSD_HEAD_DIM

    lane = lax.broadcasted_iota(jnp.int32, (L, LANES), 1)
    row = lax.broadcasted_iota(jnp.int32, (L, LANES), 0)
    dt = _softplus(dt_ref[...] + dtb_ref[...])
    dt = jnp.where((lane < n_heads) & (row < valid_rows), dt, 0.0)
    dta = dt * (-jnp.exp(alog_ref[...]))
    dt_pieces = _split3(dt)
    acum_pieces = _split3(_cumsum_rows(dta, L))
    li = lax.broadcasted_iota(jnp.int32, (L, L), 0)
    si = lax.broadcasted_iota(jnp.int32, (L, L), 1)
    tril = si <= li
    low_half = lax.broadcasted_iota(jnp.int32, (L, LANES), 1) < SSD_HEAD_DIM
    for gi in range(n_grp):
        gc = slice(gi * gw, (gi + 1) * gw)
        nc = slice(gi * D_STATE, (gi + 1) * D_STATE)
        xs = _conv_silu(xpad_ref, x_ref, cwx_ref, cbx_ref, L, gc)
        bm_g = _conv_silu(bpad_ref, b_ref, cwb_ref, cbb_ref, L, nc)
        cm_bf = _conv_silu(cpad_ref, c_ref, cwc_ref, cbc_ref, L, nc).astype(BF16)
        expand = exp_ref[:, gc]
        dt_w = _dot01_pieces(dt_pieces, expand)
        acum_w = _dot01_pieces(acum_pieces, expand)
        a_last = acum_w[L - 1:L, :]
        xdt = xs * dt_w
        acum_t = acum_w.T
        cb = _dot_nt(cm_bf, bm_g.astype(BF16))
        y_pairs = []
        for pr in range(heads // 2):
            p0 = pr * LANES
            xp = xdt[:, p0:p0 + LANES].astype(BF16)
            y_two = []
            for r0 in (p0, p0 + SSD_HEAD_DIM):
                col = acum_w[:, r0:r0 + 1]
                rowv = acum_t[r0:r0 + 1, :]
                decay = jnp.exp(jnp.where(tril, col - rowv, -jnp.inf))
                y_two.append(_dot((cb * decay).astype(BF16), xp))
            y_pairs.append(jnp.where(low_half, y_two[0], y_two[1]))
        y_diag = jnp.concatenate(y_pairs, axis=1)

        st = st_ref[:, gc]
        y_off = jnp.exp(acum_w) * _dot(cm_bf, st.astype(BF16))
        xdec = (xdt * jnp.exp(a_last - acum_w)).astype(BF16)
        st_ref[:, gc] = jnp.exp(a_last) * st + _dot(bm_g.T.astype(BF16), xdec)

        y = y_diag + y_off + xs * dsk_ref[:, gc]
        z = z_ref[:, gc]
        y = y * (z * _sigmoid(z))
        y_ref[:, gc] = _rms(y, ng_ref[:, gc]).astype(y_ref.dtype)


def _ssd_decode_kernel(pt_ref, *refs, ssd_kernel, decode_kernel, n_ssd_in, n_dec_in, n_ssd_scratch):
    ssd_in, dec_in = refs[:n_ssd_in], refs[n_ssd_in:n_ssd_in + n_dec_in]
    rest = refs[n_ssd_in + n_dec_in:]
    ssd_out, dec_out, scr = rest[:2], rest[2:3], rest[3:]
    ssd_refs = (*ssd_in, *ssd_out, *scr[:n_ssd_scratch])
    dec_refs = (pt_ref, *dec_in, *dec_out, *scr[n_ssd_scratch:])
    for phase in PHASES:
        ssd_kernel(*ssd_refs, phases=(phase,))
        decode_kernel(*dec_refs, phases=(phase,))


def ssd_core(zx, conv_init, h0, layer, conv_w, conv_b, dt_bias, a_log, d_skip, norm_g,
             *, batch, seq, valid_rows, groups_per_step, decode=None):
    n_heads = dt_bias.shape[0]
    d_inner = n_heads * SSD_HEAD_DIM
    gw = d_inner // SSD_GROUPS
    gps = groups_per_step
    wx = gps * gw
    wn = gps * D_STATE
    hs = gps * (n_heads // SSD_GROUPS)
    L = min(SSD_CHUNK, seq)
    nc = seq // L
    xb0 = d_inner // wx
    bb0 = (2 * d_inner) // wn
    cb0 = (2 * d_inner + SSD_GROUPS * D_STATE) // wn
    dtb0 = (2 * d_inner + 2 * SSD_GROUPS * D_STATE) // LANES
    cwb0 = d_inner // wn
    cwc0 = (d_inner + SSD_GROUPS * D_STATE) // wn

    pad_h = LANES - n_heads
    dtb = jnp.pad(dt_bias.astype(F32), (0, pad_h)).reshape(1, LANES)
    alog = jnp.pad(a_log.astype(F32), (0, pad_h)).reshape(1, LANES)
    dsk = jnp.repeat(d_skip.astype(F32), SSD_HEAD_DIM).reshape(1, d_inner)
    head_of_lane = jnp.arange(d_inner) // SSD_HEAD_DIM
    expand = (jnp.arange(LANES)[:, None] == head_of_lane[None, :]).astype(BF16)
    conv_b2 = conv_b.reshape(1, -1)

    row_blk = lambda b, g, c, *_: b * nc + c
    kern = functools.partial(_ssd_kernel, L=L, valid_rows=valid_rows, n_heads=n_heads, gw=gw)
    n_gsteps = SSD_GROUPS // gps
    grid = (batch, n_gsteps, nc)
    in_specs = [
            pl.BlockSpec((L, wx), lambda b, g, c, *_: (row_blk(b, g, c), g)),
            pl.BlockSpec((L, wx), lambda b, g, c, *_: (row_blk(b, g, c), xb0 + g)),
            pl.BlockSpec((L, wn), lambda b, g, c, *_: (row_blk(b, g, c), bb0 + g)),
            pl.BlockSpec((L, wn), lambda b, g, c, *_: (row_blk(b, g, c), cb0 + g)),
            pl.BlockSpec((L, LANES), lambda b, g, c, *_: (row_blk(b, g, c), dtb0)),
            pl.BlockSpec((CONV_W, wx), lambda b, g, c, *_: (0, g)),
            pl.BlockSpec((CONV_W, wn), lambda b, g, c, *_: (0, cwb0 + g)),
            pl.BlockSpec((CONV_W, wn), lambda b, g, c, *_: (0, cwc0 + g)),
            pl.BlockSpec((1, wx), lambda b, g, c, *_: (0, g)),
            pl.BlockSpec((1, wn), lambda b, g, c, *_: (0, cwb0 + g)),
            pl.BlockSpec((1, wn), lambda b, g, c, *_: (0, cwc0 + g)),
            pl.BlockSpec((None, SUBLANES, wx), lambda b, g, c, *_: (b, 0, g)),
            pl.BlockSpec((None, SUBLANES, wn), lambda b, g, c, *_: (b, 0, cwb0 + g)),
            pl.BlockSpec((None, SUBLANES, wn), lambda b, g, c, *_: (b, 0, cwc0 + g)),
            pl.BlockSpec((1, LANES), lambda b, g, c, *_: (0, 0)),
            pl.BlockSpec((1, LANES), lambda b, g, c, *_: (0, 0)),
            pl.BlockSpec((1, wx), lambda b, g, c, *_: (0, g)),
            pl.BlockSpec((1, wx), lambda b, g, c, *_: (0, g)),
            pl.BlockSpec((LANES, wx), lambda b, g, c, *_: (0, g)),
            pl.BlockSpec((None, None, hs, SSD_HEAD_DIM, D_STATE), lambda b, g, c, *_: (layer, b, g, 0, 0)),
    ]
    out_specs = [
        pl.BlockSpec((L, wx), lambda b, g, c, *_: (row_blk(b, g, c), g)),
        pl.BlockSpec((None, hs, SSD_HEAD_DIM, D_STATE), lambda b, g, c, *_: (b, g, 0, 0)),
    ]
    out_shape = [
        jax.ShapeDtypeStruct((batch * seq, d_inner), BF16),
        jax.ShapeDtypeStruct((batch, n_heads, SSD_HEAD_DIM, D_STATE), F32),
    ]
    scratch = [
        pltpu.VMEM((L + SUBLANES, wx), F32),
        pltpu.VMEM((L + SUBLANES, wn), F32),
        pltpu.VMEM((L + SUBLANES, wn), F32),
        pltpu.VMEM((D_STATE, wx), F32),
    ]
    args = (zx, zx, zx, zx, zx, conv_w, conv_w, conv_w, conv_b2, conv_b2, conv_b2,
            conv_init, conv_init, conv_init, dtb, alog, dsk, norm_g.reshape(1, d_inner), expand, h0)
    if decode is None:
        y, hout = pl.pallas_call(
            kern, grid=grid, in_specs=in_specs, out_specs=out_specs, out_shape=out_shape, scratch_shapes=scratch,
            compiler_params=_cparams(("parallel", "parallel", "arbitrary")), name="ssd_core",
        )(*args)
        return y, hout
    span = batch * n_gsteps // decode.batch
    assert (batch * n_gsteps, span * nc) == (decode.batch * span, decode.n_steps) and n_gsteps % span == 0
    dec = decode(lambda b, g, c, *_: ((b * n_gsteps + g) // span, (g % span) * nc + c), len(grid))
    fused = functools.partial(_ssd_decode_kernel, ssd_kernel=kern, decode_kernel=dec.kernel,
                              n_ssd_in=len(in_specs), n_dec_in=len(dec.in_specs), n_ssd_scratch=len(scratch))
    grid_spec = pltpu.PrefetchScalarGridSpec(
        num_scalar_prefetch=1, grid=grid, in_specs=in_specs + dec.in_specs,
        out_specs=out_specs + [dec.out_spec], scratch_shapes=scratch + dec.scratch)
    y, hout, att = pl.pallas_call(
        fused, grid_spec=grid_spec, out_shape=out_shape + [dec.out_shape],
        compiler_params=_cparams(("arbitrary", "arbitrary", "arbitrary")), name="ssd_core_fox_decode",
    )(dec.page_table, *args, *dec.args)
    return y, hout, dec.finish(att)


def _forget_kernel(p_ref, bf_ref, logf_ref, fcum_ref, *, seq, n_heads):
    lane = lax.broadcasted_iota(jnp.int32, (seq, LANES), 1)
    logit = p_ref[...] + bf_ref[...]
    logf = jnp.where(lane < n_heads, -_softplus(-logit), 0.0)
    logf_ref[...] = logf[:, :n_heads]
    fcum_ref[...] = _cumsum_rows(logf, seq)


def fox_forget(flog, b_f, *, batch, seq):
    n_heads = b_f.shape[0]
    bfp = jnp.pad(b_f.astype(F32), (0, LANES - n_heads)).reshape(1, LANES)
    return pl.pallas_call(
        functools.partial(_forget_kernel, seq=seq, n_heads=n_heads),
        grid=(batch,),
        in_specs=[pl.BlockSpec((seq, LANES), lambda b: (b, 0)),
                  pl.BlockSpec((1, LANES), lambda b: (0, 0))],
        out_specs=[pl.BlockSpec((seq, n_heads), lambda b: (b, 0)),
                   pl.BlockSpec((seq, LANES), lambda b: (b, 0))],
        out_shape=[jax.ShapeDtypeStruct((batch * seq, n_heads), F32),
                   jax.ShapeDtypeStruct((batch * seq, LANES), F32)],
        compiler_params=_cparams(("parallel",)),
        name="fox_forget",
    )(flog, bfp)


N_PIECES = 3


def _fox_flash_kernel(q_ref, k_ref, v_ref, f_ref, o_ref, qa_ref, ka_ref, vb_ref, *, seq, tq):
    h = pl.program_id(1)
    hd = FOX_HEAD_DIM
    c1 = LOG2E * hd ** -0.5
    pieces = jnp.concatenate(_split3(f_ref[...] * hd ** 0.5), axis=1)
    r = lax.broadcasted_iota(jnp.int32, (N_PIECES * LANES, LANES), 0)
    c = lax.broadcasted_iota(jnp.int32, (N_PIECES * LANES, LANES), 1)
    is_h = (r % LANES) == h
    sel_q = jnp.where(is_h & (c == r // LANES), 1.0, 0.0).astype(BF16)
    sel_k = jnp.where(is_h & (c == N_PIECES + r // LANES), -1.0, 0.0).astype(BF16)
    lane = lax.broadcasted_iota(jnp.int32, (seq, LANES), 1)
    ones_q = jnp.where((lane >= N_PIECES) & (lane < 2 * N_PIECES), 1.0, 0.0)
    ones_k = jnp.where(lane < N_PIECES, 1.0, 0.0)
    qa_ref[:, 0:hd] = q_ref[...].astype(BF16)
    qa_ref[:, hd:2 * hd] = (_dot(pieces, sel_q) + ones_q).astype(BF16)
    ka_ref[:, 0:hd] = k_ref[...].astype(BF16)
    ka_ref[:, hd:2 * hd] = (_dot(pieces, sel_k) + ones_k).astype(BF16)
    vb_ref[...] = v_ref[...].astype(BF16)

    ti = lax.broadcasted_iota(jnp.int32, (tq, tq), 0)
    si = lax.broadcasted_iota(jnp.int32, (tq, tq), 1)
    for n in range(seq // tq):
        r0, r1 = n * tq, (n + 1) * tq
        qa = qa_ref[r0:r1, :]
        u_d = jnp.where(si <= ti, _dot_nt(qa, ka_ref[r0:r1, :]), -jnp.inf)
        m = jnp.max(u_d, axis=1, keepdims=True)
        if n > 0:
            u_o = _dot_nt(qa, ka_ref[0:r0, :])
            m = jnp.maximum(m, jnp.max(u_o, axis=1, keepdims=True))
        p_d = jnp.exp2((u_d - m) * c1)
        l = jnp.sum(p_d, axis=1, keepdims=True)
        acc = _dot(p_d.astype(BF16), vb_ref[r0:r1, :])
        if n > 0:
            p_o = jnp.exp2((u_o - m) * c1)
            l = l + jnp.sum(p_o, axis=1, keepdims=True)
            acc = acc + _dot(p_o.astype(BF16), vb_ref[0:r0, :])
        o_ref[r0:r1, :] = (acc / l).astype(o_ref.dtype)


def fox_flash(q, k_stack, v_stack, slot, fcum, *, batch, seq, n_heads, tq):
    hd = FOX_HEAD_DIM
    kern = functools.partial(_fox_flash_kernel, seq=seq, tq=tq)
    return pl.pallas_call(
        kern,
        grid=(batch, n_heads),
        in_specs=[
            pl.BlockSpec((seq, hd), lambda b, h: (b, h)),
            pl.BlockSpec((None, seq, hd), lambda b, h: (slot, b, h)),
            pl.BlockSpec((None, seq, hd), lambda b, h: (slot, b, h)),
            pl.BlockSpec((seq, LANES), lambda b, h: (b, 0)),
        ],
        out_specs=pl.BlockSpec((seq, hd), lambda b, h: (b, h)),
        out_shape=jax.ShapeDtypeStruct((batch * seq, n_heads * hd), BF16),
        scratch_shapes=[pltpu.VMEM((seq, 2 * hd), BF16), pltpu.VMEM((seq, 2 * hd), BF16),
                        pltpu.VMEM((seq, hd), BF16)],
        compiler_params=_cparams(("parallel", "parallel")),
        name="fox_flash",
    )(q, k_stack, v_stack, fcum)


def _past_bias_kernel(pt_ref, u_ref, *rest, pages_per_step):
    pp = pages_per_step
    lf_refs = rest[0:pp]
    o_ref = rest[pp]
    run_ref = rest[pp + 1]

    @pl.when(pl.program_id(1) == 0)
    def _():
        run_ref[...] = jnp.zeros_like(run_ref)

    for p in range(pp):
        lf = lf_refs[p][...]
        su = _dot01(lf, u_ref[...])
        run = run_ref[...]
        o_ref[pp - 1 - p] = run + su[:, :LANES]
        run_ref[...] = run + su[:, LANES:]


def fox_past_bias(cache_logf_t, page_table, layer, *, pages_per_step):
    batch, n_pages = page_table.shape
    n_heads = cache_logf_t.shape[2]
    pp = pages_per_step
    nblk = n_pages // pp
    pos = jnp.arange(PAGE_SIZE)
    u = jnp.concatenate([(pos[:, None] > pos[None, :]), jnp.ones((PAGE_SIZE, PAGE_SIZE), bool)],
                        axis=1).astype(BF16)

    def page_map(p):
        return lambda b, j, pt: (layer, pt[b, n_pages - 1 - (j * pp + p)], 0, 0)

    grid_spec = pltpu.PrefetchScalarGridSpec(
        num_scalar_prefetch=1,
        grid=(batch, nblk),
        in_specs=[pl.BlockSpec((PAGE_SIZE, 2 * PAGE_SIZE), lambda b, j, pt: (0, 0))]
        + [pl.BlockSpec((None, None, n_heads, PAGE_SIZE), page_map(p)) for p in range(pp)],
        out_specs=pl.BlockSpec((None, pp, n_heads, PAGE_SIZE), lambda b, j, pt: (b, nblk - 1 - j, 0, 0)),
        scratch_shapes=[pltpu.VMEM((n_heads, PAGE_SIZE), F32)],
    )
    return pl.pallas_call(
        functools.partial(_past_bias_kernel, pages_per_step=pp),
        grid_spec=grid_spec,
        out_shape=jax.ShapeDtypeStruct((batch, n_pages, n_heads, PAGE_SIZE), F32),
        compiler_params=_cparams(("parallel", "arbitrary")),
        name="fox_past_bias",
    )(page_table, u, *([cache_logf_t] * pp))


def _decode_kernel(pt_ref, q_ref, kn_ref, vn_ref, fcol_ref, frow_ref, trow_ref, *rest, n_heads, pages_per_step,
                   step_of, n_axes, n_steps, phases=PHASES):
    pp = pages_per_step
    k_refs = rest[0:pp]
    v_refs = rest[pp:2 * pp]
    o_ref = rest[2 * pp]
    m_ref, l_ref, acc_ref, mb_ref, qb_ref = rest[2 * pp + 1:]
    j = step_of(*[pl.program_id(a) for a in range(n_axes)])[1]
    c1 = LOG2E * FOX_HEAD_DIM ** -0.5
    bias_scale = FOX_HEAD_DIM ** 0.5

    if "init" in phases:
        pl.when(j == 0)(functools.partial(_decode_init, q_ref, kn_ref, vn_ref, fcol_ref, frow_ref, m_ref, l_ref,
                                          acc_ref, mb_ref, qb_ref, n_heads=n_heads, c1=c1, bias_scale=bias_scale))
    if "main" in phases:
        _decode_pages(trow_ref, k_refs, v_refs, m_ref, l_ref, acc_ref, mb_ref, qb_ref, c1=c1, bias_scale=bias_scale)
    if "final" in phases:
        @pl.when(j == n_steps - 1)
        def _():
            o_ref[...] = (acc_ref[...] / l_ref[...]).astype(o_ref.dtype)


def _decode_init(q_ref, kn_ref, vn_ref, fcol_ref, frow_ref, m_ref, l_ref, acc_ref, mb_ref, qb_ref, *, n_heads, c1,
                 bias_scale):
    rows, cols = mb_ref.shape
    r = lax.broadcasted_iota(jnp.int32, (rows, cols), 0)
    c = lax.broadcasted_iota(jnp.int32, (rows, cols), 1)
    same_head = (r % n_heads) == (c % n_heads)
    mb_ref[...] = jnp.where(same_head, fcol_ref[...] * bias_scale, -jnp.inf)
    qb = q_ref[...].astype(BF16)
    qb_ref[...] = qb
    s = _dot_nt(qb, kn_ref[...].astype(BF16)) + mb_ref[:, 0:rows] - frow_ref[...] * bias_scale
    rn = lax.broadcasted_iota(jnp.int32, (rows, rows), 0)
    cn = lax.broadcasted_iota(jnp.int32, (rows, rows), 1)
    s = jnp.where((cn // n_heads) <= (rn // n_heads), s, -jnp.inf)
    m_new = jnp.max(s, axis=1, keepdims=True)
    p = jnp.exp2((s - m_new) * c1)
    m_ref[...] = m_new
    l_ref[...] = jnp.sum(p, axis=1, keepdims=True)
    acc_ref[...] = _dot(p.astype(BF16), vn_ref[...].astype(BF16))


def _decode_pages(trow_ref, k_refs, v_refs, m_ref, l_ref, acc_ref, mb_ref, qb_ref, *, c1, bias_scale):
    for g0 in range(0, len(k_refs), DECODE_PAGE_GROUP):
        pages = range(g0, min(g0 + DECODE_PAGE_GROUP, len(k_refs)))
        s_pages = [_dot_nt(qb_ref[...], k_refs[pg][...].astype(BF16)) + mb_ref[...] + trow_ref[pg] * bias_scale
                   for pg in pages]
        m_old = m_ref[...]
        m_new = m_old
        for s in s_pages:
            m_new = jnp.maximum(m_new, jnp.max(s, axis=1, keepdims=True))
        alpha = jnp.exp2((m_old - m_new) * c1)
        l_new = alpha * l_ref[...]
        acc_new = alpha * acc_ref[...]
        for pg, s in zip(pages, s_pages):
            p = jnp.exp2((s - m_new) * c1)
            l_new = l_new + jnp.sum(p, axis=1, keepdims=True)
            acc_new = acc_new + _dot(p.astype(BF16), v_refs[pg][...].astype(BF16))
        l_ref[...] = l_new
        acc_ref[...] = acc_new
        m_ref[...] = m_new


def _decode_parts(q, k_new, v_new, fnew, past_bias, cache_k, cache_v, page_table, layer, *, batch, n_new, n_heads,
                  pages_per_step):
    hd = FOX_HEAD_DIM
    d = n_heads * hd
    n_pages = page_table.shape[1]
    pp = pages_per_step
    rows = n_new * n_heads
    cols = PAGE_SIZE * n_heads
    q2 = q.reshape(batch, rows, hd)
    kn2 = k_new.reshape(batch, rows, hd)
    vn2 = v_new.reshape(batch, rows, hd)
    fcol = fnew.reshape(batch, rows, 1)
    frow = fnew.reshape(batch, 1, rows)
    trow = jnp.swapaxes(past_bias, 2, 3).reshape(batch, n_pages, 1, cols)

    n_steps = n_pages // pp

    def build(step_of, n_axes):
        seq_map = lambda *ids: (step_of(*ids)[0], 0, 0)

        def page_map(p):
            def index(*ids):
                b, j = step_of(*ids)
                return (layer, ids[-1][b, j * pp + p], 0, 0)
            return index

        in_specs = [
            pl.BlockSpec((None, rows, hd), seq_map),
            pl.BlockSpec((None, rows, hd), seq_map),
            pl.BlockSpec((None, rows, hd), seq_map),
            pl.BlockSpec((None, rows, 1), seq_map),
            pl.BlockSpec((None, 1, rows), seq_map),
            pl.BlockSpec((None, pp, 1, cols), lambda *ids: (*step_of(*ids), 0, 0)),
        ]
        in_specs += [pl.BlockSpec((None, None, cols, hd), page_map(p)) for p in range(pp)]
        in_specs += [pl.BlockSpec((None, None, cols, hd), page_map(p)) for p in range(pp)]
        return types.SimpleNamespace(
            kernel=functools.partial(_decode_kernel, n_heads=n_heads, pages_per_step=pp, step_of=step_of,
                                     n_axes=n_axes, n_steps=n_steps),
            batch=batch, n_steps=n_steps, page_table=page_table,
            in_specs=in_specs,
            args=(q2, kn2, vn2, fcol, frow, trow, *([cache_k] * pp), *([cache_v] * pp)),
            out_spec=pl.BlockSpec((None, rows, hd), seq_map),
            out_shape=jax.ShapeDtypeStruct((batch, rows, hd), BF16),
            scratch=[
                pltpu.VMEM((rows, 1), F32),
                pltpu.VMEM((rows, 1), F32),
                pltpu.VMEM((rows, hd), F32),
                pltpu.VMEM((rows, cols), F32),
                pltpu.VMEM((rows, hd), BF16),
            ],
            finish=lambda out: out.reshape(batch * n_new, d),
        )

    build.batch, build.n_steps = batch, n_steps
    return build


def fox_decode(*args, **kwargs):
    dec = _decode_parts(*args, **kwargs)(lambda b, j, *_: (b, j), 2)
    grid_spec = pltpu.PrefetchScalarGridSpec(
        num_scalar_prefetch=1, grid=(dec.batch, dec.n_steps), in_specs=dec.in_specs, out_specs=dec.out_spec,
        scratch_shapes=dec.scratch)
    out = pl.pallas_call(
        dec.kernel, grid_spec=grid_spec, out_shape=dec.out_shape,
        compiler_params=_cparams(("parallel", "arbitrary")), name="fox_decode",
    )(dec.page_table, *dec.args)
    return dec.finish(out)


def kernel(x_prompt, x_sample, cache_k, cache_v, cache_logf, state_conv, state_ssm, page_table, ln_mix_pre, ln_mix_post, ln_ffn_pre, ln_ffn_post, ssd_w_in, ssd_conv_w, ssd_conv_b, ssd_dt_bias, ssd_a_log, ssd_d, ssd_norm, ssd_w_out, fox_w_in, fox_b_f, fox_w_out, ffn_w_gate, ffn_w_up, ffn_w_down):
    bp, lp, d = x_prompt.shape
    bs, ls, _ = x_sample.shape
    depth = ln_mix_pre.shape[0]
    n_att, n_pool, page, fox_heads, fox_hd = cache_k.shape
    ssd_heads = ssd_dt_bias.shape[1]
    d_inner = ssd_heads * SSD_HEAD_DIM
    conv_dim = ssd_conv_w.shape[2]
    mp, ms = bp * lp, bs * ls
    tmp, tms = min(mp, 1024), min(ms, 1024)

    hp = x_prompt.reshape(mp, d)
    hs = x_sample.reshape(ms, d)
    ck = cache_k.reshape(n_att, n_pool, page * fox_heads, fox_hd)
    cv = cache_v.reshape(n_att, n_pool, page * fox_heads, fox_hd)
    clf_t = jnp.swapaxes(cache_logf, 2, 3)

    zero_conv = jnp.zeros((bp, SUBLANES, conv_dim), F32)
    zero_ssm = jnp.zeros((1, bp, ssd_heads, SSD_HEAD_DIM, D_STATE), F32)
    conv_init_s = jnp.pad(state_conv, ((0, 0), (0, 0), (SUBLANES - (CONV_W - 1), 0), (0, 0)))
    ls_pad = SSD_CHUNK

    w_ssd_out, w_fox_out = ssd_w_out.astype(BF16), fox_w_out.astype(BF16)

    fp_l, cp_l, sp_l = [], [], []
    fs_l, cs_l, ss_l = [], [], []
    kst_p = vst_p = kst_s = vst_s = None
    gps = SSD_GROUPS_PER_STEP
    ffn_s = lambda h, i: ffn(h, ln_ffn_pre[i], ffn_w_gate, ffn_w_up, ffn_w_down, i, ln_ffn_post[i], tm=tms, tf=512)
    ffn_p = lambda h, i, w: ffn(h, ln_ffn_pre[i], *w, 0, ln_ffn_post[i], tm=min(mp, 512), tf=512)
    for i in range(0, depth, 2):
        j = i // 2
        par = (ssd_conv_w[j], ssd_conv_b[j], ssd_dt_bias[j], ssd_a_log[j], ssd_d[j], ssd_norm[j])
        zx_s, w_ssd_in = norm_matmul(hs, ln_mix_pre[i], ssd_w_in, j, tm=tms, tn=SSD_IN_TN)
        zx_s3 = zx_s.reshape(bs, ls, -1)
        zx_sp = jnp.pad(zx_s3, ((0, 0), (0, ls_pad - ls), (0, 0))).reshape(bs * ls_pad, -1)
        y_sp, st_s = ssd_core(zx_sp, conv_init_s[j], state_ssm, j, *par, batch=bs, seq=ls_pad, valid_rows=ls,
                              groups_per_step=SSD_GROUPS)
        y_s = y_sp.reshape(bs, ls_pad, d_inner)[:, :ls].reshape(ms, d_inner)
        hs = matmul_norm_res(y_s, w_ssd_out, j, ln_mix_post[i], hs, tm=tms)
        hs, *w_ffn = ffn_s(hs, i)
        q_s, kst_s, vst_s, fl_s, w_fox_in, w_fox_inf = fox_in_proj(
            hs, ln_mix_pre[i + 1], fox_w_in, fox_w_in, j, kst_s, vst_s, j, n_att, tm=tms, tn=FOX_IN_TN,
            n_heads=fox_heads)
        lf_s, fc_s = fox_forget(fl_s, fox_b_f[j], batch=bs, seq=ls)
        tb_s = fox_past_bias(clf_t, page_table, j, pages_per_step=32)
        decode = _decode_parts(q_s, kst_s[j], vst_s[j], fc_s[:, :fox_heads], tb_s, ck, cv, page_table, j, batch=bs,
                               n_new=ls, n_heads=fox_heads, pages_per_step=4)
        zx_p = norm_matmul(hp, ln_mix_pre[i], w_ssd_in, 0, tm=tmp, tn=SSD_IN_TN)
        y_p, st_p, a_s = ssd_core(zx_p, zero_conv, zero_ssm, 0, *par, batch=bp, seq=lp, valid_rows=SSD_CHUNK,
                                  groups_per_step=gps, decode=decode)
        hp = matmul_norm_res(y_p, w_ssd_out, j, ln_mix_post[i], hp, tm=min(mp, 256))
        hp = ffn_p(hp, i, w_ffn)
        cp_l.append(zx_p.reshape(bp, lp, -1)[:, lp - (CONV_W - 1):, d_inner:d_inner + conv_dim])
        cs_l.append(zx_s3[:, ls - (CONV_W - 1):, d_inner:d_inner + conv_dim])
        sp_l.append(st_p)
        ss_l.append(st_s)
        hs = matmul_norm_res(a_s, w_fox_out, j, ln_mix_post[i + 1], hs, tm=tms)
        hs, *w_ffn = ffn_s(hs, i + 1)
        q_p, kst_p, vst_p, fl_p = fox_in_proj(
            hp, ln_mix_pre[i + 1], w_fox_in, w_fox_inf, 0, kst_p, vst_p, j, n_att, tm=min(mp, 1024), tn=512,
            n_heads=fox_heads)
        lf_p, fc_p = fox_forget(fl_p, fox_b_f[j], batch=bp, seq=lp)
        a_p = fox_flash(q_p, kst_p, vst_p, j, fc_p, batch=bp, seq=lp, n_heads=fox_heads, tq=256)
        hp = matmul_norm_res(a_p, w_fox_out, j, ln_mix_post[i + 1], hp, tm=min(mp, 512))
        hp = ffn_p(hp, i + 1, w_ffn)
        fp_l.append(lf_p.reshape(bp, lp, fox_heads))
        fs_l.append(lf_s.reshape(bs, ls, fox_heads))
    kv_p = lambda a: a.reshape(n_att, bp, lp, fox_heads, fox_hd)
    kv_s = lambda a: a.reshape(n_att, bs, ls, fox_heads, fox_hd)
    return (hp.reshape(bp, lp, d), hs.reshape(bs, ls, d),
            kv_p(kst_p), kv_p(vst_p), jnp.stack(fp_l), jnp.stack(cp_l), jnp.stack(sp_l),
            kv_s(kst_s), kv_s(vst_s), jnp.stack(fs_l), jnp.stack(cs_l), jnp.stack(ss_l))
```
